```python
import math
import jax, jax.numpy as jnp
from jax import lax
import numpy as np

D_MODEL = 1024
BATCH = 8
SEQ = 4096
DEPTH = 2
DEC_BATCH = 16
DEC_SEQ = 64
PAST_LEN = 4096

CHUNK = 64
N_EVEN = (DEPTH + 1) // 2
N_ODD = DEPTH // 2
EPS = 1e-6
D_FF = 2816
POOL_WINDOWS = (2, 4, 8, 16)
POOL_GROUPS = 4
POOL_GW = 96
POOL_W = POOL_GROUPS * POOL_GW
POOL_PAD = max(POOL_WINDOWS) - 1
HEAD_DIM = 64
SWA_KV_HEADS = 2
SWA_GROUP = 8
SWA_HEADS = SWA_KV_HEADS * SWA_GROUP
SWA_Q_W = SWA_HEADS * HEAD_DIM
SWA_KV_W = SWA_KV_HEADS * HEAD_DIM
WINDOW = 128
WIN_CHUNKS = WINDOW // CHUNK
EVEN_IN = POOL_W + SWA_Q_W + 2 * SWA_KV_W
EVEN_MIX = POOL_W + SWA_Q_W
DIFF_HEADS = 8
DIFF_DH = 2 * HEAD_DIM
DIFF_W = DIFF_HEADS * DIFF_DH
QBLK = 128
GMLP_GROUPS = 4
GMLP_GW = 96
GMLP_W = GMLP_GROUPS * GMLP_GW
GMLP_CHUNK = 128
ODD_IN = 3 * DIFF_W + 2 * GMLP_W
ODD_MIX = DIFF_W + GMLP_W

kernel_name = 'hybrid_pool_swa_diff_gmlp_stream_step'


def rmsnorm(x, g):
    xf = x.astype(jnp.float32)
    y = xf * lax.rsqrt(jnp.mean(xf * xf, axis=-1, keepdims=True) + EPS)
    return (y * g.astype(jnp.float32)).astype(x.dtype)


def layernorm(x, g, b):
    xf = x.astype(jnp.float32)
    mu = jnp.mean(xf, axis=-1, keepdims=True)
    var = jnp.mean(jnp.square(xf - mu), axis=-1, keepdims=True)
    y = (xf - mu) * lax.rsqrt(var + EPS) * g.astype(jnp.float32) + b.astype(jnp.float32)
    return y.astype(x.dtype)


def swiglu(h, wg, wu, wd):
    return (jax.nn.silu(h @ wg) * (h @ wu)) @ wd


def pool_mixer(u, prev, pos0, w_grp, scale):
    b, L, _ = u.shape
    p = jnp.concatenate([prev, u], axis=1)
    pf = p.astype(jnp.float32)
    cs = jnp.concatenate([jnp.zeros((b, 1, POOL_W), jnp.float32), jnp.cumsum(pf, axis=1)], axis=1)
    uf = pf[:, POOL_PAD:]
    pos = pos0 + jnp.arange(L)
    diffs = []
    for g, w in enumerate(POOL_WINDOWS):
        sl = slice(g * POOL_GW, (g + 1) * POOL_GW)
        s = cs[:, POOL_PAD + 1:POOL_PAD + 1 + L, sl] - cs[:, POOL_PAD + 1 - w:POOL_PAD + 1 - w + L, sl]
        cnt = jnp.minimum(pos + 1, w).astype(jnp.float32)
        diffs.append(s / cnt[None, :, None] - uf[..., sl])
    d = jnp.stack(diffs, axis=2)
    y = jnp.einsum('blgc,gcd->blgd', d, w_grp.astype(jnp.float32)).reshape(b, L, POOL_W)
    y = y * scale.astype(jnp.float32)
    return y.astype(u.dtype), p[:, -POOL_PAD:]


def band_keys(t, nc):
    b = t.shape[0]
    tp = jnp.pad(t, ((0, 0), (WINDOW, 0), (0, 0), (0, 0))).reshape(b, nc + WIN_CHUNKS, CHUNK, SWA_KV_HEADS, HEAD_DIM)
    return jnp.concatenate([tp[:, i:i + nc] for i in range(WIN_CHUNKS + 1)], axis=2)


def swa_core(q, k, v, key_ok, sink):
    s = jnp.einsum('bnqhgd,bnkhd->bnhgqk', q, k).astype(jnp.float32) * (HEAD_DIM ** -0.5)
    s = jnp.where(key_ok[None, :, None, None, None, :], s, -jnp.inf)
    sk = jnp.broadcast_to(sink.astype(jnp.float32)[None, None, :, :, None, None], s.shape[:-1] + (1,))
    p = jax.nn.softmax(jnp.concatenate([s, sk], axis=-1), axis=-1)[..., :-1]
    return jnp.einsum('bnhgqk,bnkhd->bnqhgd', p.astype(v.dtype), v)


def even_mixer(h, w_in, w_out, pool_w, pool_scale, sink, cache_pool, cache_k, cache_v):
    b, L, _ = h.shape
    u, q, k, v = jnp.split(h @ w_in, [POOL_W, POOL_W + SWA_Q_W, POOL_W + SWA_Q_W + SWA_KV_W], axis=-1)
    q = q.reshape(b, L, SWA_KV_HEADS, SWA_GROUP, HEAD_DIM)
    k = k.reshape(b, L, SWA_KV_HEADS, HEAD_DIM)
    v = v.reshape(b, L, SWA_KV_HEADS, HEAD_DIM)
    if cache_k is None:
        prev = jnp.zeros((b, POOL_PAD, POOL_W), u.dtype)
        pos0 = 0
        nc = L // CHUNK
        qb = q.reshape(b, nc, CHUNK, SWA_KV_HEADS, SWA_GROUP, HEAD_DIM)
        kb = band_keys(k, nc)
        vb = band_keys(v, nc)
        key_pos = jnp.arange(nc)[:, None] * CHUNK - WINDOW + jnp.arange((WIN_CHUNKS + 1) * CHUNK)[None, :]
        key_ok = key_pos >= 0
        k_all, v_all = k, v
    else:
        prev = cache_pool
        pos0 = PAST_LEN
        k_all = jnp.concatenate([cache_k, k], axis=1)
        v_all = jnp.concatenate([cache_v, v], axis=1)
        qb, kb, vb = q[:, None], k_all[:, None], v_all[:, None]
        key_ok = jnp.ones((1, k_all.shape[1]), dtype=bool)
    a_out, new_pool = pool_mixer(u, prev, pos0, pool_w, pool_scale)
    b_out = swa_core(qb, kb, vb, key_ok, sink.reshape(SWA_KV_HEADS, SWA_GROUP)).reshape(b, L, SWA_Q_W)
    y = jnp.concatenate([a_out, b_out], axis=-1) @ w_out
    return y, new_pool, k_all[:, -WINDOW:], v_all[:, -WINDOW:]


def diff_core(q, k, v, key_ok, lam, subln_g):
    def amap(qq, kk):
        s = jnp.einsum('bqhd,bkhd->bhqk', qq, kk).astype(jnp.float32) * (HEAD_DIM ** -0.5)
        return jax.nn.softmax(jnp.where(key_ok[None, None], s, -jnp.inf), axis=-1)
    a = amap(q[..., :HEAD_DIM], k[..., :HEAD_DIM]) - lam * amap(q[..., HEAD_DIM:], k[..., HEAD_DIM:])
    o = jnp.einsum('bhqk,bkhd->bqhd', a.astype(v.dtype), v)
    return rmsnorm(o, subln_g)


def odd_mixer(h, w_in, w_out, lam_p, subln_g, ln_g, ln_b, w_s, b_s, cache_k, cache_v, layer_idx):
    b, L, _ = h.shape
    q, k, v, uz, vz = jnp.split(h @ w_in, [DIFF_W, 2 * DIFF_W, 3 * DIFF_W, 3 * DIFF_W + GMLP_W], axis=-1)
    q = q.reshape(b, L, DIFF_HEADS, DIFF_DH)
    k = k.reshape(b, L, DIFF_HEADS, DIFF_DH)
    v = v.reshape(b, L, DIFF_HEADS, DIFF_DH)
    lam_init = 0.8 - 0.6 * math.exp(-0.3 * layer_idx)
    lp = lam_p.astype(jnp.float32)
    lam = jnp.exp(jnp.sum(lp[0] * lp[1])) - jnp.exp(jnp.sum(lp[2] * lp[3])) + lam_init
    if cache_k is None:
        nq = L // QBLK
        kchunk = jnp.arange(L) // CHUNK
        qb = q.reshape(b, nq, QBLK, DIFF_HEADS, DIFF_DH).swapaxes(0, 1)

        def blk(args):
            qi, i = args
            qchunk = (i * QBLK + jnp.arange(QBLK)) // CHUNK
            return diff_core(qi, k, v, kchunk[None, :] <= qchunk[:, None], lam, subln_g)

        o = lax.map(blk, (qb, jnp.arange(nq))).swapaxes(0, 1).reshape(b, L, DIFF_W)
        lc = GMLP_CHUNK
    else:
        kc = jnp.concatenate([cache_k, k], axis=1)
        vc = jnp.concatenate([cache_v, v], axis=1)
        qchunk = (PAST_LEN + jnp.arange(L)) // CHUNK
        kchunk = jnp.arange(kc.shape[1]) // CHUNK
        o = diff_core(q, kc, vc, kchunk[None, :] <= qchunk[:, None], lam, subln_g).reshape(b, L, DIFF_W)
        lc = L
    o = o * (1.0 - lam_init)
    u = jax.nn.gelu(uz, approximate=False)
    vg = layernorm(jax.nn.gelu(vz, approximate=False), ln_g, ln_b)
    nb = L // lc
    vr = vg.reshape(b, nb, lc, GMLP_GROUPS, GMLP_GW)
    ws = jnp.tril(w_s[:, :lc, :lc])
    mix = jnp.einsum('gts,bnsgc->bntgc', ws, vr) + b_s[:, :lc].T[None, None, :, :, None]
    d_out = u * mix.reshape(b, L, GMLP_W)
    y = jnp.concatenate([o, d_out], axis=-1) @ w_out
    return y, k, v, vg


def setup_inputs(seed: int = 0) -> dict:
    key = jax.random.key(seed)
    ks = jax.random.split(key, 25)

    def nrm(k, shape, s):
        return jax.random.normal(k, shape, jnp.float32) * s

    return {
        'x_prompt': nrm(ks[0], (BATCH, SEQ, D_MODEL), 1.0),
        'x_sample': nrm(ks[1], (DEC_BATCH, DEC_SEQ, D_MODEL), 1.0),
        'cache_pool': nrm(ks[2], (N_EVEN, DEC_BATCH, POOL_PAD, POOL_W), 1.0),
        'cache_swa_k': nrm(ks[3], (N_EVEN, DEC_BATCH, WINDOW, SWA_KV_HEADS, HEAD_DIM), 1.0),
        'cache_swa_v': nrm(ks[4], (N_EVEN, DEC_BATCH, WINDOW, SWA_KV_HEADS, HEAD_DIM), 1.0),
        'cache_diff_k': nrm(ks[5], (N_ODD, DEC_BATCH, PAST_LEN, DIFF_HEADS, DIFF_DH), 1.0),
        'cache_diff_v': nrm(ks[6], (N_ODD, DEC_BATCH, PAST_LEN, DIFF_HEADS, DIFF_DH), 1.0),
        'norm_g': 1.0 + nrm(ks[7], (DEPTH, 3, D_MODEL), 0.02),
        'final_g': 1.0 + nrm(ks[8], (D_MODEL,), 0.02),
        'ffn_gate': nrm(ks[9], (DEPTH, 2, D_MODEL, D_FF), D_MODEL ** -0.5),
        'ffn_up': nrm(ks[10], (DEPTH, 2, D_MODEL, D_FF), D_MODEL ** -0.5),
        'ffn_down': nrm(ks[11], (DEPTH, 2, D_FF, D_MODEL), D_FF ** -0.5),
        'even_w_in': nrm(ks[12], (N_EVEN, D_MODEL, EVEN_IN), D_MODEL ** -0.5),
        'even_w_out': nrm(ks[13], (N_EVEN, EVEN_MIX, D_MODEL), EVEN_MIX ** -0.5),
        'pool_w': nrm(ks[14], (N_EVEN, POOL_GROUPS, POOL_GW, POOL_GW), POOL_GW ** -0.5),
        'pool_scale': 1.0 + nrm(ks[15], (N_EVEN, POOL_W), 0.1),
        'swa_sink': nrm(ks[16], (N_EVEN, SWA_HEADS), 1.0),
        'odd_w_in': nrm(ks[17], (N_ODD, D_MODEL, ODD_IN), D_MODEL ** -0.5),
        'odd_w_out': nrm(ks[18], (N_ODD, ODD_MIX, D_MODEL), ODD_MIX ** -0.5),
        'diff_lambda': nrm(ks[19], (N_ODD, 4, HEAD_DIM), 0.1),
        'diff_subln_g': 1.0 + nrm(ks[20], (N_ODD, DIFF_DH), 0.02),
        'gmlp_ln_g': 1.0 + nrm(ks[21], (N_ODD, GMLP_W), 0.02),
        'gmlp_ln_b': nrm(ks[22], (N_ODD, GMLP_W), 0.02),
        'gmlp_w_s': nrm(ks[23], (N_ODD, GMLP_GROUPS, GMLP_CHUNK, GMLP_CHUNK), GMLP_CHUNK ** -0.5),
        'gmlp_b_s': 1.0 + nrm(ks[24], (N_ODD, GMLP_GROUPS, GMLP_CHUNK), 0.1),
    }


def reference(x_prompt, x_sample, cache_pool, cache_swa_k, cache_swa_v, cache_diff_k, cache_diff_v,
              norm_g, final_g, ffn_gate, ffn_up, ffn_down, even_w_in, even_w_out, pool_w, pool_scale,
              swa_sink, odd_w_in, odd_w_out, diff_lambda, diff_subln_g, gmlp_ln_g, gmlp_ln_b,
              gmlp_w_s, gmlp_b_s):
    def trunk(x, prompt):
        pool_s, swa_k, swa_v, diff_k, diff_v, gmlp_v = [], [], [], [], [], []
        for l in range(DEPTH):
            j = l // 2
            x = x + 0.5 * swiglu(rmsnorm(x, norm_g[l, 0]), ffn_gate[l, 0], ffn_up[l, 0], ffn_down[l, 0])
            h = rmsnorm(x, norm_g[l, 1])
            if l % 2 == 0:
                y, sp, sk, sv = even_mixer(h, even_w_in[j], even_w_out[j], pool_w[j], pool_scale[j], swa_sink[j],
                                           None if prompt else cache_pool[j],
                                           None if prompt else cache_swa_k[j],
                                           None if prompt else cache_swa_v[j])
                pool_s.append(sp)
                swa_k.append(sk)
                swa_v.append(sv)
            else:
                y, dk, dv, gv = odd_mixer(h, odd_w_in[j], odd_w_out[j], diff_lambda[j], diff_subln_g[j],
                                          gmlp_ln_g[j], gmlp_ln_b[j], gmlp_w_s[j], gmlp_b_s[j],
                                          None if prompt else cache_diff_k[j],
                                          None if prompt else cache_diff_v[j], l)
                diff_k.append(dk)
                diff_v.append(dv)
                if not prompt:
                    gmlp_v.append(gv)
            x = x + y
            x = x + 0.5 * swiglu(rmsnorm(x, norm_g[l, 2]), ffn_gate[l, 1], ffn_up[l, 1], ffn_down[l, 1])
        return rmsnorm(x, final_g), pool_s, swa_k, swa_v, diff_k, diff_v, gmlp_v

    y_p, pool_p, swa_k_p, swa_v_p, diff_k_p, diff_v_p, _ = trunk(x_prompt, True)
    y_s, pool_s, swa_k_s, swa_v_s, diff_k_s, diff_v_s, gmlp_v_s = trunk(x_sample, False)
    return (y_p, y_s,
            jnp.stack(pool_p), jnp.stack(pool_s),
            jnp.stack(swa_k_p), jnp.stack(swa_k_s),
            jnp.stack(swa_v_p), jnp.stack(swa_v_s),
            jnp.stack(diff_k_p), jnp.stack(diff_k_s),
            jnp.stack(diff_v_p), jnp.stack(diff_v_s),
            jnp.stack(gmlp_v_s))
```

```python
import functools
import math

import jax
import jax.numpy as jnp
from jax import lax
from jax.experimental import pallas as pl
from jax.experimental.pallas import tpu as pltpu

F32 = jnp.float32
BF16 = jnp.bfloat16

EPS = 1e-6
CHUNK = 64
HEAD_DIM = 64
POOL_WINDOWS = (2, 4, 8, 16)
POOL_GW = 96
POOL_W = 4 * POOL_GW
POOL_HALO = 16
SWA_WINDOW = 128
SWA_KEYS = 256
SWA_HEADS = 16
SWA_Q_W = SWA_HEADS * HEAD_DIM
SWA_KV_W = 2 * HEAD_DIM
DIFF_HEADS = 8
DIFF_DH = 2 * HEAD_DIM
DIFF_W = DIFF_HEADS * DIFF_DH
GMLP_GROUPS = 4
GMLP_GW = 96
GMLP_W = GMLP_GROUPS * GMLP_GW
SM_SCALE = HEAD_DIM ** -0.5

V7X_VMEM_LIMIT_BYTES = 56 * 1024 * 1024
NEG_INF = float("-inf")


def _rms(x, g):
    return x * lax.rsqrt(jnp.mean(x * x, axis=-1, keepdims=True) + EPS) * g


def _dot(a, b):
    return jnp.dot(a, b, preferred_element_type=F32)


def _dot_nt(a, b):
    return lax.dot_general(a, b, (((1,), (1,)), ((), ())), preferred_element_type=F32)


def _resident(shape):
    nd = len(shape)
    return pl.BlockSpec(shape, lambda *_: (0,) * nd, pipeline_mode=pl.Buffered(1))


def _row_tile(n_rows, want):
    t = min(want, n_rows)
    assert n_rows % t == 0, (n_rows, t)
    return t


def _ffn_kernel(*refs, ff_chunk, has_mix, has_final):
    it = iter(refs)
    x_ref = next(it)
    if has_mix:
        o_ref, d_ref, woa_ref, wob_ref = next(it), next(it), next(it), next(it)
    g_ref, wg_ref, wu_ref, wd_ref = next(it), next(it), next(it), next(it)
    if has_final:
        fg_ref = next(it)
    out_ref = next(it)
    acc_ref = next(it)

    x = x_ref[...]
    if has_mix:
        x = x + _dot(o_ref[...], woa_ref[...]) + _dot(d_ref[...], wob_ref[...])
    hb = _rms(x, g_ref[...]).astype(BF16)
    d_ff = wg_ref.shape[1]
    for c in range(d_ff // ff_chunk):
        sl = slice(c * ff_chunk, (c + 1) * ff_chunk)
        gate = _dot(hb, wg_ref[:, sl])
        up = _dot(hb, wu_ref[:, sl])
        act = (gate * jax.nn.sigmoid(gate) * up).astype(BF16)
        part = _dot(act, wd_ref[sl, :])
        if c == 0:
            acc_ref[...] = part
        else:
            acc_ref[...] += part
    y = x + 0.5 * acc_ref[...]
    if has_final:
        y = _rms(y, fg_ref[...])
    out_ref[...] = y


def _ffn(x, g, wg, wu, wd, mix=None, final_g=None, *, name):
    t, d = x.shape
    d_ff = wg.shape[1]
    tm = _row_tile(t, 512)
    ff_chunk = 256
    assert d_ff % ff_chunk == 0
    row = lambda w: pl.BlockSpec((tm, w), lambda i: (i, 0))
    args, specs = [x], [row(d)]
    if mix is not None:
        o, dd, woa, wob = mix
        args += [o, dd, woa, wob]
        specs += [row(o.shape[1]), row(dd.shape[1]), _resident(woa.shape), _resident(wob.shape)]
    args += [g.reshape(1, d), wg, wu, wd]
    specs += [_resident((1, d)), _resident(wg.shape), _resident(wu.shape), _resident(wd.shape)]
    if final_g is not None:
        args.append(final_g.reshape(1, d))
        specs.append(_resident((1, d)))
    kern = functools.partial(_ffn_kernel, ff_chunk=ff_chunk, has_mix=mix is not None,
                             has_final=final_g is not None)
    return pl.pallas_call(
        kern,
        grid=(t // tm,),
        in_specs=specs,
        out_specs=row(d),
        out_shape=jax.ShapeDtypeStruct((t, d), F32),
        scratch_shapes=[pltpu.VMEM((tm, d), F32)],
        compiler_params=pltpu.CompilerParams(dimension_semantics=("arbitrary",),
                                             vmem_limit_bytes=V7X_VMEM_LIMIT_BYTES),
        name=name,
    )(*args)


def _even_kernel(sink_ref, x_ref, g_ref, win_ref, pw_ref, ps_ref, wout_ref, cpool_ref, ckv_ref,
                 xo_ref, pool_o_ref, kv_o_ref,
                 e_ref, kv_ref, qb_ref, bo_ref, *, ts, tq, pos0, has_cache):
    sb = pl.program_id(1)
    n_sub = ts // tq
    kv_rows = kv_ref.shape[0]

    @pl.when(sb == 0)
    def _():
        e_ref[0:POOL_HALO, :] = cpool_ref[0]
        kv_ref[0:SWA_WINDOW, :] = ckv_ref[0]
        if kv_rows > SWA_WINDOW + ts:
            kv_ref[SWA_WINDOW + ts:, :] = jnp.zeros((kv_rows - SWA_WINDOW - ts, 2 * SWA_KV_W), F32)

    x = x_ref[0]
    hb = _rms(x, g_ref[...]).astype(BF16)
    u = _dot(hb, win_ref[:, 0:POOL_W])
    qb_ref[...] = (_dot(hb, win_ref[:, POOL_W:POOL_W + SWA_Q_W]) * SM_SCALE).astype(BF16)
    kv_ref[SWA_WINDOW:SWA_WINDOW + ts, :] = _dot(hb, win_ref[:, POOL_W + SWA_Q_W:])
    e_ref[POOL_HALO:POOL_HALO + ts, :] = u

    ev = e_ref[...]
    s2 = ev + pltpu.roll(ev, 1, 0)
    s4 = s2 + pltpu.roll(s2, 2, 0)
    s8 = s4 + pltpu.roll(s4, 4, 0)
    s16 = s8 + pltpu.roll(s8, 8, 0)
    lane = lax.broadcasted_iota(jnp.int32, (1, POOL_W), 1)
    g0, g1, g2 = lane < POOL_GW, lane < 2 * POOL_GW, lane < 3 * POOL_GW
    sw = jnp.where(g0, s2, jnp.where(g1, s4, jnp.where(g2, s8, s16)))[POOL_HALO:]
    win = jnp.where(g0, POOL_WINDOWS[0], jnp.where(g1, POOL_WINDOWS[1],
                                                  jnp.where(g2, POOL_WINDOWS[2], POOL_WINDOWS[3])))
    pos = pos0 + sb * ts + lax.broadcasted_iota(jnp.int32, (ts, 1), 0)
    cnt = jnp.minimum(pos + 1, win).astype(F32)
    diff = sw / cnt - u
    a_out = _dot(diff.astype(BF16), pw_ref[...]) * ps_ref[...]
    new_halo = ev[ts:ts + POOL_HALO]
    e_ref[0:POOL_HALO, :] = new_halo
    pool_o_ref[0] = new_halo

    lane128 = lax.broadcasted_iota(jnp.int32, (1, 2 * HEAD_DIM), 1)
    lo = lane128 < HEAD_DIM
    rq = lax.broadcasted_iota(jnp.int32, (tq, 2 * SWA_KEYS), 0) // CHUNK
    col = lax.broadcasted_iota(jnp.int32, (tq, 2 * SWA_KEYS), 1)
    ek = jnp.where(col >= SWA_KEYS, col - SWA_KEYS, col)
    ekc = ek // CHUNK
    band = (rq <= ekc) & (ekc <= rq + SWA_WINDOW // CHUNK)

    def halves(xa, xb):
        return jnp.concatenate([jnp.where(lo, xa, 0.0), jnp.where(lo, 0.0, xb)], axis=0).astype(BF16)

    def sub_block(a, carry):
        r0 = pl.multiple_of(a * tq, tq)
        kve = kv_ref[pl.ds(r0, SWA_KEYS), :]
        ke, ve = kve[:, :SWA_KV_W], kve[:, SWA_KV_W:]
        ke_r = pltpu.roll(ke, HEAD_DIM, 1)
        ve_r = pltpu.roll(ve, HEAD_DIM, 1)
        k2 = (halves(ke, ke_r), halves(ke_r, ke))
        v2 = (halves(ve, ve_r), halves(ve_r, ve))
        if has_cache:
            valid = band
        else:
            valid = band & ((ek >= SWA_WINDOW) | (sb * n_sub + a > 0))
        for j in range(SWA_HEADS // 2):
            hh = (2 * j) // (SWA_HEADS // 2)
            qp = qb_ref[pl.ds(r0, tq), 128 * j:128 * (j + 1)]
            s = jnp.where(valid, _dot_nt(qp, k2[hh]), NEG_INF)
            ps, dens = [], []
            for half in range(2):
                sh = s[:, half * SWA_KEYS:(half + 1) * SWA_KEYS]
                sink = sink_ref[2 * j + half]
                m = jnp.maximum(jnp.max(sh, axis=-1, keepdims=True), sink)
                p = jnp.exp(sh - m)
                ps.append(p)
                dens.append(jnp.sum(p, axis=-1, keepdims=True) + jnp.exp(sink - m))
            p = jnp.concatenate(ps, axis=1).astype(BF16)
            o = _dot(p, v2[hh])
            o = o / jnp.where(lo, dens[0], dens[1])
            bo_ref[pl.ds(r0, tq), 128 * j:128 * (j + 1)] = o.astype(BF16)
        return carry

    lax.fori_loop(0, n_sub, sub_block, 0)

    y = _dot(a_out.astype(BF16), wout_ref[0:POOL_W, :]) + _dot(bo_ref[...], wout_ref[POOL_W:, :])
    xo_ref[0] = x + y

    new_win = kv_ref[ts:ts + SWA_WINDOW, :]
    kv_ref[0:SWA_WINDOW, :] = new_win
    kv_o_ref[0] = new_win


def _even_mixer(x, g, w_in, pool_wbd, pool_scale, sink, w_out, cache_pool, cache_kv, *, pos0, has_cache, name):
    b, s, d = x.shape
    ts = _row_tile(s, 512)
    tq = min(ts, SWA_WINDOW)
    kv_rows = max(SWA_WINDOW + ts, (ts // tq - 1) * tq + SWA_KEYS)
    kern = functools.partial(_even_kernel, ts=ts, tq=tq, pos0=pos0, has_cache=has_cache)
    per_b = lambda r, w: pl.BlockSpec((1, r, w), lambda bi, si: (bi, 0, 0))
    return pl.pallas_call(
        kern,
        grid=(b, s // ts),
        in_specs=[
            pl.BlockSpec(memory_space=pltpu.SMEM),
            pl.BlockSpec((1, ts, d), lambda bi, si: (bi, si, 0)),
            _resident((1, d)), _resident(w_in.shape), _resident(pool_wbd.shape), _resident((1, POOL_W)),
            _resident(w_out.shape),
            per_b(POOL_HALO, POOL_W), per_b(SWA_WINDOW, 2 * SWA_KV_W),
        ],
        out_specs=[
            pl.BlockSpec((1, ts, d), lambda bi, si: (bi, si, 0)),
            per_b(POOL_HALO, POOL_W), per_b(SWA_WINDOW, 2 * SWA_KV_W),
        ],
        out_shape=[
            jax.ShapeDtypeStruct((b, s, d), F32),
            jax.ShapeDtypeStruct((b, POOL_HALO, POOL_W), F32),
            jax.ShapeDtypeStruct((b, SWA_WINDOW, 2 * SWA_KV_W), F32),
        ],
        scratch_shapes=[
            pltpu.VMEM((POOL_HALO + ts, POOL_W), F32),
            pltpu.VMEM((kv_rows, 2 * SWA_KV_W), F32),
            pltpu.VMEM((ts, SWA_Q_W), BF16),
            pltpu.VMEM((ts, SWA_Q_W), BF16),
        ],
        compiler_params=pltpu.CompilerParams(dimension_semantics=("arbitrary", "arbitrary"),
                                             vmem_limit_bytes=V7X_VMEM_LIMIT_BYTES),
        name=name,
    )(sink, x, g.reshape(1, d), w_in, pool_wbd, pool_scale.reshape(1, POOL_W), w_out, cache_pool, cache_kv)


def _gelu(x):
    return 0.5 * x * (1.0 + lax.erf(x * math.sqrt(0.5)))


def _odd_proj_kernel(*refs, lc, with_vg):
    (x_ref, g_ref, w_ref, lng_ref, lnb_ref, wcat_ref, bm_ref,
     q_ref, k_ref, v_ref, kb_ref, vb_ref, d_ref) = refs[:13]
    vg_ref = refs[13] if with_vg else None
    tm = x_ref.shape[0]

    hb = _rms(x_ref[...], g_ref[...]).astype(BF16)
    q_ref[...] = (_dot(hb, w_ref[:, 0:DIFF_W]) * SM_SCALE).astype(BF16)
    k = _dot(hb, w_ref[:, DIFF_W:2 * DIFF_W])
    k_ref[...] = k
    kb_ref[...] = k.astype(BF16)
    v = _dot(hb, w_ref[:, 2 * DIFF_W:3 * DIFF_W])
    v_ref[...] = v
    vb_ref[...] = v.astype(BF16)

    u = _gelu(_dot(hb, w_ref[:, 3 * DIFF_W:3 * DIFF_W + GMLP_W]))
    gv = _gelu(_dot(hb, w_ref[:, 3 * DIFF_W + GMLP_W:]))
    mu = jnp.mean(gv, axis=-1, keepdims=True)
    cen = gv - mu
    var = jnp.mean(cen * cen, axis=-1, keepdims=True)
    vg = cen * lax.rsqrt(var + EPS) * lng_ref[...] + lnb_ref[...]
    if with_vg:
        vg_ref[...] = vg

    t_i = lax.broadcasted_iota(jnp.int32, (lc, GMLP_GROUPS * lc), 0)
    s_i = lax.broadcasted_iota(jnp.int32, (lc, GMLP_GROUPS * lc), 1) % lc
    wcat = jnp.where(s_i <= t_i, wcat_ref[...], 0.0).astype(BF16)
    lane = lax.broadcasted_iota(jnp.int32, (1, GMLP_W), 1) // GMLP_GW
    for c in range(tm // lc):
        rows = slice(c * lc, (c + 1) * lc)
        vc = vg[rows]
        vstack = jnp.concatenate([jnp.where(lane == gi, vc, 0.0) for gi in range(GMLP_GROUPS)],
                                 axis=0).astype(BF16)
        mix = _dot(wcat, vstack) + bm_ref[...]
        d_ref[rows, :] = (u[rows] * mix).astype(BF16)


def _odd_proj(x, g, w_in, ln_g, ln_b, wcat, bias_m, *, lc, with_vg, name):
    t, d = x.shape
    tm = _row_tile(t, 512)
    assert tm % lc == 0
    row = lambda w: pl.BlockSpec((tm, w), lambda i: (i, 0))
    out_shape = [
        jax.ShapeDtypeStruct((t, DIFF_W), BF16),
        jax.ShapeDtypeStruct((t, DIFF_W), F32),
        jax.ShapeDtypeStruct((t, DIFF_W), F32),
        jax.ShapeDtypeStruct((t, DIFF_W), BF16),
        jax.ShapeDtypeStruct((t, DIFF_W), BF16),
        jax.ShapeDtypeStruct((t, GMLP_W), BF16),
    ]
    out_specs = [row(DIFF_W)] * 5 + [row(GMLP_W)]
    if with_vg:
        out_shape.append(jax.ShapeDtypeStruct((t, GMLP_W), F32))
        out_specs.append(row(GMLP_W))
    return pl.pallas_call(
        functools.partial(_odd_proj_kernel, lc=lc, with_vg=with_vg),
        grid=(t // tm,),
        in_specs=[row(d), _resident((1, d)), _resident(w_in.shape), _resident((1, GMLP_W)),
                  _resident((1, GMLP_W)), _resident(wcat.shape), _resident(bias_m.shape)],
        out_specs=out_specs,
        out_shape=out_shape,
        compiler_params=pltpu.CompilerParams(dimension_semantics=("arbitrary",),
                                             vmem_limit_bytes=V7X_VMEM_LIMIT_BYTES),
        name=name,
    )(x, g.reshape(1, d), w_in, ln_g.reshape(1, GMLP_W), ln_b.reshape(1, GMLP_W), wcat, bias_m)


def _diff_attn_kernel(*refs, tq, tkc, n_cache, lam_init):
    if n_cache:
        lam_ref, sg_ref, q_ref, k_ref, v_ref, ck_ref, cv_ref, o_ref, m_ref, l_ref, acc_ref = refs
    else:
        lam_ref, sg_ref, q_ref, k_ref, v_ref, o_ref, m_ref, l_ref, acc_ref = refs
    s_len = q_ref.shape[1]
    nq = s_len // tq

    lp = lam_ref[...]
    lam = (jnp.exp(jnp.sum(lp[0:1] * lp[1:2], axis=-1, keepdims=True))
           - jnp.exp(jnp.sum(lp[2:3] * lp[3:4], axis=-1, keepdims=True)) + lam_init)
    lo = lax.broadcasted_iota(jnp.int32, (1, DIFF_DH), 1) < HEAD_DIM
    diag_ok = (lax.broadcasted_iota(jnp.int32, (2 * tq, tq), 1) // CHUNK
               <= (lax.broadcasted_iota(jnp.int32, (2 * tq, tq), 0) % tq) // CHUNK)

    def q_block(i, carry):
        q0 = pl.multiple_of(i * tq, tq)
        qi = q_ref[0, pl.ds(q0, tq), :]
        qs = jnp.concatenate([jnp.where(lo, qi, 0.0), jnp.where(lo, 0.0, qi)], axis=0).astype(BF16)
        m_ref[...] = jnp.full(m_ref.shape, NEG_INF, F32)
        l_ref[...] = jnp.zeros(l_ref.shape, F32)
        acc_ref[...] = jnp.zeros(acc_ref.shape, F32)

        def kv_step(kb, vb, mask):
            s = _dot_nt(qs, kb)
            if mask is not None:
                s = jnp.where(mask, s, NEG_INF)
            m_prev = m_ref[...]
            m_new = jnp.maximum(m_prev, jnp.max(s, axis=-1, keepdims=True))
            alpha = jnp.exp(m_prev - m_new)
            p = jnp.exp(s - m_new)
            l_ref[...] = alpha * l_ref[...] + jnp.sum(p, axis=-1, keepdims=True)
            acc_ref[...] = alpha * acc_ref[...] + _dot(p.astype(BF16), vb)
            m_ref[...] = m_new

        if n_cache:
            def cache_step(j, c):
                r0 = pl.multiple_of(j * tkc, tkc)
                kv_step(ck_ref[0, pl.ds(r0, tkc), :].astype(BF16),
                        cv_ref[0, pl.ds(r0, tkc), :].astype(BF16), None)
                return c
            lax.fori_loop(0, n_cache, cache_step, 0)

        def past_step(j, c):
            r0 = pl.multiple_of(j * tq, tq)
            kv_step(k_ref[0, pl.ds(r0, tq), :], v_ref[0, pl.ds(r0, tq), :], None)
            return c
        lax.fori_loop(0, i, past_step, 0)
        kv_step(k_ref[0, pl.ds(q0, tq), :], v_ref[0, pl.ds(q0, tq), :], diag_ok)

        o = acc_ref[...] / l_ref[...]
        o = o[:tq] - lam * o[tq:]
        o = _rms(o, sg_ref[...]) * (1.0 - lam_init)
        o_ref[0, pl.ds(q0, tq), :] = o.astype(BF16)
        return carry

    lax.fori_loop(0, nq, q_block, 0)


def _diff_attn(q, kb, vb, cache_k, cache_v, lam_p, subln_g, *, lam_init, name):
    b, s, _ = q.shape
    tq = _row_tile(s, 256)
    per_head = lambda rows: pl.BlockSpec((1, rows, DIFF_DH), lambda bi, hi: (bi, 0, hi))
    args = [lam_p, subln_g.reshape(1, DIFF_DH), q, kb, vb]
    specs = [_resident(lam_p.shape), _resident((1, DIFF_DH)), per_head(s), per_head(s), per_head(s)]
    n_cache, tkc = 0, 0
    if cache_k is not None:
        lc = cache_k.shape[1]
        tkc = _row_tile(lc, 512)
        n_cache = lc // tkc
        args += [cache_k, cache_v]
        specs += [per_head(lc), per_head(lc)]
    return pl.pallas_call(
        functools.partial(_diff_attn_kernel, tq=tq, tkc=tkc, n_cache=n_cache, lam_init=lam_init),
        grid=(b, DIFF_HEADS),
        in_specs=specs,
        out_specs=per_head(s),
        out_shape=jax.ShapeDtypeStruct((b, s, DIFF_W), BF16),
        scratch_shapes=[pltpu.VMEM((2 * tq, 1), F32), pltpu.VMEM((2 * tq, 1), F32),
                        pltpu.VMEM((2 * tq, DIFF_DH), F32)],
        compiler_params=pltpu.CompilerParams(dimension_semantics=("arbitrary", "arbitrary"),
                                             vmem_limit_bytes=V7X_VMEM_LIMIT_BYTES),
        name=name,
    )(*args)


def _block_diag(w):
    g, a, bb = w.shape
    eye = jnp.eye(g, dtype=w.dtype)
    return (eye[:, None, :, None] * w[:, :, None, :]).reshape(g * a, g * bb)


def _trunk(x, prompt, caches, wts, tag):
    cache_pool, cache_swa_k, cache_swa_v, cache_diff_k, cache_diff_v = caches
    b, s, d = x.shape
    depth = wts["norm_g"].shape[0]
    norm_g = wts["norm_g"]
    xt = x.reshape(b * s, d)
    pool_s, swa_k, swa_v, diff_k, diff_v, gmlp_v = [], [], [], [], [], []
    pending_mix = None
    past_len = 0 if prompt else cache_diff_k.shape[2]

    for l in range(depth):
        j = l // 2
        f0, f1 = wts["ffn"][l]
        xt = _ffn(xt, norm_g[l, 0], *f0, mix=pending_mix, name=f"{tag}_ffn{l}a")
        pending_mix = None
        if l % 2 == 0:
            if prompt:
                cpool = jnp.zeros((b, POOL_HALO, POOL_W), F32)
                ckv = jnp.zeros((b, SWA_WINDOW, 2 * SWA_KV_W), F32)
            else:
                cpool = jnp.pad(cache_pool[j], ((0, 0), (POOL_HALO - cache_pool.shape[2], 0), (0, 0)))
                ckv = jnp.concatenate([cache_swa_k[j].reshape(b, SWA_WINDOW, SWA_KV_W),
                                       cache_swa_v[j].reshape(b, SWA_WINDOW, SWA_KV_W)], axis=-1)
            x3, pool_o, kv_o = _even_mixer(
                xt.reshape(b, s, d), norm_g[l, 1], wts["even_w_in"][j], wts["pool_wbd"][j],
                wts["pool_scale"][j], wts["swa_sink"][j], wts["even_w_out"][j], cpool, ckv,
                pos0=past_len, has_cache=not prompt, name=f"{tag}_even{l}")
            xt = x3.reshape(b * s, d)
            pool_s.append(pool_o[:, 1:])
            swa_k.append(kv_o[..., :SWA_KV_W].reshape(b, SWA_WINDOW, 2, HEAD_DIM))
            swa_v.append(kv_o[..., SWA_KV_W:].reshape(b, SWA_WINDOW, 2, HEAD_DIM))
            xt = _ffn(xt, norm_g[l, 2], *f1, name=f"{tag}_ffn{l}b")
        else:
            lc = 128 if prompt else s
            outs = _odd_proj(xt, norm_g[l, 1], wts["odd_w_in"][j], wts["gmlp_ln_g"][j], wts["gmlp_ln_b"][j],
                             wts["gmlp_wcat"][(j, lc)], wts["gmlp_bias"][(j, lc)],
                             lc=lc, with_vg=not prompt, name=f"{tag}_oddproj{l}")
            q, k, v, kb, vb, dd = outs[:6]
            diff_k.append(k.reshape(b, s, DIFF_HEADS, DIFF_DH))
            diff_v.append(v.reshape(b, s, DIFF_HEADS, DIFF_DH))
            if not prompt:
                gmlp_v.append(outs[6].reshape(b, s, GMLP_W))
                ck = cache_diff_k[j].reshape(b, past_len, DIFF_W)
                cv = cache_diff_v[j].reshape(b, past_len, DIFF_W)
            else:
                ck = cv = None
            lam_init = 0.8 - 0.6 * math.exp(-0.3 * l)
            o = _diff_attn(q.reshape(b, s, DIFF_W), kb.reshape(b, s, DIFF_W), vb.reshape(b, s, DIFF_W),
                           ck, cv, wts["diff_lambda"][j], wts["diff_subln_g"][j],
                           lam_init=lam_init, name=f"{tag}_diffattn{l}")
            w_out = wts["odd_w_out"][j]
            mix = (o.reshape(b * s, DIFF_W), dd, w_out[:DIFF_W], w_out[DIFF_W:])
            last = l == depth - 1
            xt = _ffn(xt, norm_g[l, 2], *f1, mix=mix, final_g=wts["final_g"] if last else None,
                      name=f"{tag}_ffn{l}b")
            pending_mix = None
    if depth % 2 == 1:
        raise NotImplementedError("odd depth")
    return xt.reshape(b, s, d), pool_s, swa_k, swa_v, diff_k, diff_v, gmlp_v


def _prep_weights(norm_g, final_g, ffn_gate, ffn_up, ffn_down, even_w_in, even_w_out, pool_w, pool_scale,
                  swa_sink, odd_w_in, odd_w_out, diff_lambda, diff_subln_g, gmlp_ln_g, gmlp_ln_b,
                  gmlp_w_s, gmlp_b_s, gmlp_chunks):
    depth = norm_g.shape[0]
    ffn = [[(ffn_gate[l, i].astype(BF16), ffn_up[l, i].astype(BF16), ffn_down[l, i].astype(BF16))
            for i in range(2)] for l in range(depth)]
    n_even, n_odd = even_w_in.shape[0], odd_w_in.shape[0]
    wcat, bias = {}, {}
    for j in range(n_odd):
        for lc in gmlp_chunks:
            ws = gmlp_w_s[j][:, :lc, :lc]
            wcat[(j, lc)] = jnp.transpose(ws, (1, 0, 2)).reshape(lc, GMLP_GROUPS * lc)
            bias[(j, lc)] = jnp.repeat(gmlp_b_s[j][:, :lc].T, GMLP_GW, axis=1)
    return dict(
        norm_g=norm_g, final_g=final_g, ffn=ffn,
        even_w_in=[even_w_in[j].astype(BF16) for j in range(n_even)],
        even_w_out=[even_w_out[j].astype(BF16) for j in range(n_even)],
        pool_wbd=[_block_diag(pool_w[j]).astype(BF16) for j in range(n_even)],
        pool_scale=pool_scale, swa_sink=swa_sink,
        odd_w_in=[odd_w_in[j].astype(BF16) for j in range(n_odd)],
        odd_w_out=[odd_w_out[j].astype(BF16) for j in range(n_odd)],
        diff_lambda=diff_lambda, diff_subln_g=diff_subln_g, gmlp_ln_g=gmlp_ln_g, gmlp_ln_b=gmlp_ln_b,
        gmlp_wcat=wcat, gmlp_bias=bias,
    )


def kernel(x_prompt, x_sample, cache_pool, cache_swa_k, cache_swa_v, cache_diff_k, cache_diff_v, norm_g, final_g, ffn_gate, ffn_up, ffn_down, even_w_in, even_w_out, pool_w, pool_scale, swa_sink, odd_w_in, odd_w_out, diff_lambda, diff_subln_g, gmlp_ln_g, gmlp_ln_b, gmlp_w_s, gmlp_b_s):
    wts = _prep_weights(norm_g, final_g, ffn_gate, ffn_up, ffn_down, even_w_in, even_w_out, pool_w,
                        pool_scale, swa_sink, odd_w_in, odd_w_out, diff_lambda, diff_subln_g, gmlp_ln_g,
                        gmlp_ln_b, gmlp_w_s, gmlp_b_s, gmlp_chunks=(128, x_sample.shape[1]))
    caches = (cache_pool, cache_swa_k, cache_swa_v, cache_diff_k, cache_diff_v)
    y_p, pool_p, swa_k_p, swa_v_p, diff_k_p, diff_v_p, _ = _trunk(x_prompt, True, caches, wts, "p")
    y_s, pool_s, swa_k_s, swa_v_s, diff_k_s, diff_v_s, gmlp_v_s = _trunk(x_sample, False, caches, wts, "s")
    return (y_p, y_s,
            jnp.stack(pool_p), jnp.stack(pool_s),
            jnp.stack(swa_k_p), jnp.stack(swa_k_s),
            jnp.stack(swa_v_p), jnp.stack(swa_v_s),
            jnp.stack(diff_k_p), jnp.stack(diff_k_s),
            jnp.stack(diff_v_p), jnp.stack(diff_v_s),
            jnp.stack(gmlp_v_s))
```

```python
import functools
import math

import jax
import jax.numpy as jnp
from jax import lax
from jax.experimental import pallas as pl
from jax.experimental.pallas import tpu as pltpu

F32 = jnp.float32
BF16 = jnp.bfloat16

EPS = 1e-6
CHUNK = 64
HEAD_DIM = 64
POOL_WINDOWS = (2, 4, 8, 16)
POOL_GW = 96
POOL_W = 4 * POOL_GW
POOL_HALO = 16
SWA_WINDOW = 128
SWA_KEYS = 256
SWA_HEADS = 16
SWA_Q_W = SWA_HEADS * HEAD_DIM
SWA_KV_W = 2 * HEAD_DIM
DIFF_HEADS = 8
DIFF_DH = 2 * HEAD_DIM
DIFF_W = DIFF_HEADS * DIFF_DH
GMLP_GROUPS = 4
GMLP_GW = 96
GMLP_W = GMLP_GROUPS * GMLP_GW
SM_SCALE = HEAD_DIM ** -0.5
LOG2_E = math.log2(math.e)

V7X_VMEM_LIMIT_BYTES = 56 * 1024 * 1024
NEG_INF = float("-inf")


def _rms(x, g):
    return x * lax.rsqrt(jnp.mean(x * x, axis=-1, keepdims=True) + EPS) * g


def _dot(a, b):
    return jnp.dot(a, b, preferred_element_type=F32)


def _dot_nt(a, b):
    return lax.dot_general(a, b, (((1,), (1,)), ((), ())), preferred_element_type=F32)


def _resident(shape):
    nd = len(shape)
    return pl.BlockSpec(shape, lambda *_: (0,) * nd, pipeline_mode=pl.Buffered(1))


def _row_tile(n_rows, want):
    t = min(want, n_rows)
    assert n_rows % t == 0, (n_rows, t)
    return t


def _ffn_kernel(*refs, ff_chunk, has_mix, has_final):
    it = iter(refs)
    x_ref = next(it)
    if has_mix:
        o_ref, d_ref, woa_ref, wob_ref = next(it), next(it), next(it), next(it)
    g_ref, wg_ref, wu_ref, wd_ref = next(it), next(it), next(it), next(it)
    if has_final:
        fg_ref = next(it)
    out_ref = next(it)
    acc_ref = next(it)

    x = x_ref[...]
    if has_mix:
        x = x + _dot(o_ref[...], woa_ref[...]) + _dot(d_ref[...], wob_ref[...])
    hb = _rms(x, g_ref[...]).astype(BF16)
    d_ff = wg_ref.shape[1]
    for c in range(d_ff // ff_chunk):
        sl = slice(c * ff_chunk, (c + 1) * ff_chunk)
        gate = _dot(hb, wg_ref[:, sl])
        up = _dot(hb, wu_ref[:, sl])
        act = (gate * jax.nn.sigmoid(gate) * up).astype(BF16)
        part = _dot(act, wd_ref[sl, :])
        if c == 0:
            acc_ref[...] = part
        else:
            acc_ref[...] += part
    y = x + 0.5 * acc_ref[...]
    if has_final:
        y = _rms(y, fg_ref[...])
    out_ref[...] = y


def _ffn(x, g, wg, wu, wd, mix=None, final_g=None, *, name):
    t, d = x.shape
    d_ff = wg.shape[1]
    tm = _row_tile(t, 512)
    ff_chunk = 256
    assert d_ff % ff_chunk == 0
    row = lambda w: pl.BlockSpec((tm, w), lambda i: (i, 0))
    args, specs = [x], [row(d)]
    if mix is not None:
        o, dd, woa, wob = mix
        args += [o, dd, woa, wob]
        specs += [row(o.shape[1]), row(dd.shape[1]), _resident(woa.shape), _resident(wob.shape)]
    args += [g.reshape(1, d), wg, wu, wd]
    specs += [_resident((1, d)), _resident(wg.shape), _resident(wu.shape), _resident(wd.shape)]
    if final_g is not None:
        args.append(final_g.reshape(1, d))
        specs.append(_resident((1, d)))
    kern = functools.partial(_ffn_kernel, ff_chunk=ff_chunk, has_mix=mix is not None,
                             has_final=final_g is not None)
    return pl.pallas_call(
        kern,
        grid=(t // tm,),
        in_specs=specs,
        out_specs=row(d),
        out_shape=jax.ShapeDtypeStruct((t, d), F32),
        scratch_shapes=[pltpu.VMEM((tm, d), F32)],
        compiler_params=pltpu.CompilerParams(dimension_semantics=("arbitrary",),
                                             vmem_limit_bytes=V7X_VMEM_LIMIT_BYTES),
        name=name,
    )(*args)


def _even_kernel(sink_ref, x_ref, g_ref, win_ref, pw_ref, ps_ref, wout_ref, cpool_ref, ckv_ref,
                 xo_ref, pool_o_ref, kv_o_ref,
                 e_ref, kv_ref, qb_ref, bo_ref, *, ts, tq, pos0, has_cache):
    sb = pl.program_id(1)
    n_sub = ts // tq
    kv_rows = kv_ref.shape[0]

    @pl.when(sb == 0)
    def _():
        e_ref[0:POOL_HALO, :] = cpool_ref[0]
        kv_ref[0:SWA_WINDOW, :] = ckv_ref[0]
        if kv_rows > SWA_WINDOW + ts:
            kv_ref[SWA_WINDOW + ts:, :] = jnp.zeros((kv_rows - SWA_WINDOW - ts, 2 * SWA_KV_W), F32)

    x = x_ref[0]
    hb = _rms(x, g_ref[...]).astype(BF16)
    u = _dot(hb, win_ref[:, 0:POOL_W])
    qb_ref[...] = (_dot(hb, win_ref[:, POOL_W:POOL_W + SWA_Q_W]) * SM_SCALE).astype(BF16)
    kv_ref[SWA_WINDOW:SWA_WINDOW + ts, :] = _dot(hb, win_ref[:, POOL_W + SWA_Q_W:])
    e_ref[POOL_HALO:POOL_HALO + ts, :] = u

    ev = e_ref[...]
    s2 = ev + pltpu.roll(ev, 1, 0)
    s4 = s2 + pltpu.roll(s2, 2, 0)
    s8 = s4 + pltpu.roll(s4, 4, 0)
    s16 = s8 + pltpu.roll(s8, 8, 0)
    lane = lax.broadcasted_iota(jnp.int32, (1, POOL_W), 1)
    g0, g1, g2 = lane < POOL_GW, lane < 2 * POOL_GW, lane < 3 * POOL_GW
    sw = jnp.where(g0, s2, jnp.where(g1, s4, jnp.where(g2, s8, s16)))[POOL_HALO:]
    win = jnp.where(g0, POOL_WINDOWS[0], jnp.where(g1, POOL_WINDOWS[1],
                                                  jnp.where(g2, POOL_WINDOWS[2], POOL_WINDOWS[3])))
    pos = pos0 + sb * ts + lax.broadcasted_iota(jnp.int32, (ts, 1), 0)
    cnt = jnp.minimum(pos + 1, win).astype(F32)
    diff = sw / cnt - u
    a_out = _dot(diff.astype(BF16), pw_ref[...]) * ps_ref[...]
    new_halo = ev[ts:ts + POOL_HALO]
    e_ref[0:POOL_HALO, :] = new_halo
    pool_o_ref[0] = new_halo

    lane128 = lax.broadcasted_iota(jnp.int32, (1, 2 * HEAD_DIM), 1)
    lo = lane128 < HEAD_DIM
    rq = lax.broadcasted_iota(jnp.int32, (tq, 2 * SWA_KEYS), 0) // CHUNK
    col = lax.broadcasted_iota(jnp.int32, (tq, 2 * SWA_KEYS), 1)
    ek = jnp.where(col >= SWA_KEYS, col - SWA_KEYS, col)
    ekc = ek // CHUNK
    band = (rq <= ekc) & (ekc <= rq + SWA_WINDOW // CHUNK)

    def halves(xa, xb):
        return jnp.concatenate([jnp.where(lo, xa, 0.0), jnp.where(lo, 0.0, xb)], axis=0).astype(BF16)

    def sub_block(a, carry):
        r0 = pl.multiple_of(a * tq, tq)
        kve = kv_ref[pl.ds(r0, SWA_KEYS), :]
        ke, ve = kve[:, :SWA_KV_W], kve[:, SWA_KV_W:]
        ke_r = pltpu.roll(ke, HEAD_DIM, 1)
        ve_r = pltpu.roll(ve, HEAD_DIM, 1)
        k2 = (halves(ke, ke_r), halves(ke_r, ke))
        v2 = (halves(ve, ve_r), halves(ve_r, ve))
        if has_cache:
            valid = band
        else:
            valid = band & ((ek >= SWA_WINDOW) | (sb * n_sub + a > 0))
        for j in range(SWA_HEADS // 2):
            hh = (2 * j) // (SWA_HEADS // 2)
            qp = qb_ref[pl.ds(r0, tq), 128 * j:128 * (j + 1)]
            s = jnp.where(valid, _dot_nt(qp, k2[hh]), NEG_INF)
            ps, dens = [], []
            for half in range(2):
                sh = s[:, half * SWA_KEYS:(half + 1) * SWA_KEYS]
                sink = sink_ref[2 * j + half]
                m = jnp.maximum(jnp.max(sh, axis=-1, keepdims=True), sink)
                p = jnp.exp(sh - m)
                ps.append(p)
                dens.append(jnp.sum(p, axis=-1, keepdims=True) + jnp.exp(sink - m))
            p = jnp.concatenate(ps, axis=1).astype(BF16)
            o = _dot(p, v2[hh])
            o = o / jnp.where(lo, dens[0], dens[1])
            bo_ref[pl.ds(r0, tq), 128 * j:128 * (j + 1)] = o.astype(BF16)
        return carry

    lax.fori_loop(0, n_sub, sub_block, 0)

    y = _dot(a_out.astype(BF16), wout_ref[0:POOL_W, :]) + _dot(bo_ref[...], wout_ref[POOL_W:, :])
    xo_ref[0] = x + y

    new_win = kv_ref[ts:ts + SWA_WINDOW, :]
    kv_ref[0:SWA_WINDOW, :] = new_win
    kv_o_ref[0] = new_win


def _even_mixer(x, g, w_in, pool_wbd, pool_scale, sink, w_out, cache_pool, cache_kv, *, pos0, has_cache, name):
    b, s, d = x.shape
    ts = _row_tile(s, 512)
    tq = min(ts, SWA_WINDOW)
    kv_rows = max(SWA_WINDOW + ts, (ts // tq - 1) * tq + SWA_KEYS)
    kern = functools.partial(_even_kernel, ts=ts, tq=tq, pos0=pos0, has_cache=has_cache)
    per_b = lambda r, w: pl.BlockSpec((1, r, w), lambda bi, si: (bi, 0, 0))
    return pl.pallas_call(
        kern,
        grid=(b, s // ts),
        in_specs=[
            pl.BlockSpec(memory_space=pltpu.SMEM),
            pl.BlockSpec((1, ts, d), lambda bi, si: (bi, si, 0)),
            _resident((1, d)), _resident(w_in.shape), _resident(pool_wbd.shape), _resident((1, POOL_W)),
            _resident(w_out.shape),
            per_b(POOL_HALO, POOL_W), per_b(SWA_WINDOW, 2 * SWA_KV_W),
        ],
        out_specs=[
            pl.BlockSpec((1, ts, d), lambda bi, si: (bi, si, 0)),
            per_b(POOL_HALO, POOL_W), per_b(SWA_WINDOW, 2 * SWA_KV_W),
        ],
        out_shape=[
            jax.ShapeDtypeStruct((b, s, d), F32),
            jax.ShapeDtypeStruct((b, POOL_HALO, POOL_W), F32),
            jax.ShapeDtypeStruct((b, SWA_WINDOW, 2 * SWA_KV_W), F32),
        ],
        scratch_shapes=[
            pltpu.VMEM((POOL_HALO + ts, POOL_W), F32),
            pltpu.VMEM((kv_rows, 2 * SWA_KV_W), F32),
            pltpu.VMEM((ts, SWA_Q_W), BF16),
            pltpu.VMEM((ts, SWA_Q_W), BF16),
        ],
        compiler_params=pltpu.CompilerParams(dimension_semantics=("arbitrary", "arbitrary"),
                                             vmem_limit_bytes=V7X_VMEM_LIMIT_BYTES),
        name=name,
    )(sink, x, g.reshape(1, d), w_in, pool_wbd, pool_scale.reshape(1, POOL_W), w_out, cache_pool, cache_kv)


def _gelu(x):
    return 0.5 * x * (1.0 + lax.erf(x * math.sqrt(0.5)))


def _odd_proj_kernel(*refs, lc, with_vg):
    (x_ref, g_ref, w_ref, lng_ref, lnb_ref, wcat_ref, bm_ref,
     q_ref, k_ref, v_ref, kb_ref, vb_ref, d_ref) = refs[:13]
    vg_ref = refs[13] if with_vg else None
    tm = x_ref.shape[0]

    hb = _rms(x_ref[...], g_ref[...]).astype(BF16)
    q_ref[...] = (_dot(hb, w_ref[:, 0:DIFF_W]) * (SM_SCALE * LOG2_E)).astype(BF16)
    k = _dot(hb, w_ref[:, DIFF_W:2 * DIFF_W])
    k_ref[...] = k
    kb_ref[...] = k.astype(BF16)
    v = _dot(hb, w_ref[:, 2 * DIFF_W:3 * DIFF_W])
    v_ref[...] = v
    vb_ref[...] = v.astype(BF16)

    u = _gelu(_dot(hb, w_ref[:, 3 * DIFF_W:3 * DIFF_W + GMLP_W]))
    gv = _gelu(_dot(hb, w_ref[:, 3 * DIFF_W + GMLP_W:]))
    mu = jnp.mean(gv, axis=-1, keepdims=True)
    cen = gv - mu
    var = jnp.mean(cen * cen, axis=-1, keepdims=True)
    vg = cen * lax.rsqrt(var + EPS) * lng_ref[...] + lnb_ref[...]
    if with_vg:
        vg_ref[...] = vg

    t_i = lax.broadcasted_iota(jnp.int32, (lc, GMLP_GROUPS * lc), 0)
    s_i = lax.broadcasted_iota(jnp.int32, (lc, GMLP_GROUPS * lc), 1) % lc
    wcat = jnp.where(s_i <= t_i, wcat_ref[...], 0.0).astype(BF16)
    lane = lax.broadcasted_iota(jnp.int32, (1, GMLP_W), 1) // GMLP_GW
    for c in range(tm // lc):
        rows = slice(c * lc, (c + 1) * lc)
        vc = vg[rows]
        vstack = jnp.concatenate([jnp.where(lane == gi, vc, 0.0) for gi in range(GMLP_GROUPS)],
                                 axis=0).astype(BF16)
        mix = _dot(wcat, vstack) + bm_ref[...]
        d_ref[rows, :] = (u[rows] * mix).astype(BF16)


def _odd_proj(x, g, w_in, ln_g, ln_b, wcat, bias_m, *, lc, with_vg, name):
    t, d = x.shape
    tm = _row_tile(t, 512)
    assert tm % lc == 0
    row = lambda w: pl.BlockSpec((tm, w), lambda i: (i, 0))
    out_shape = [
        jax.ShapeDtypeStruct((t, DIFF_W), BF16),
        jax.ShapeDtypeStruct((t, DIFF_W), F32),
        jax.ShapeDtypeStruct((t, DIFF_W), F32),
        jax.ShapeDtypeStruct((t, DIFF_W), BF16),
        jax.ShapeDtypeStruct((t, DIFF_W), BF16),
        jax.ShapeDtypeStruct((t, GMLP_W), BF16),
    ]
    out_specs = [row(DIFF_W)] * 5 + [row(GMLP_W)]
    if with_vg:
        out_shape.append(jax.ShapeDtypeStruct((t, GMLP_W), F32))
        out_specs.append(row(GMLP_W))
    return pl.pallas_call(
        functools.partial(_odd_proj_kernel, lc=lc, with_vg=with_vg),
        grid=(t // tm,),
        in_specs=[row(d), _resident((1, d)), _resident(w_in.shape), _resident((1, GMLP_W)),
                  _resident((1, GMLP_W)), _resident(wcat.shape), _resident(bias_m.shape)],
        out_specs=out_specs,
        out_shape=out_shape,
        compiler_params=pltpu.CompilerParams(dimension_semantics=("arbitrary",),
                                             vmem_limit_bytes=V7X_VMEM_LIMIT_BYTES),
        name=name,
    )(x, g.reshape(1, d), w_in, ln_g.reshape(1, GMLP_W), ln_b.reshape(1, GMLP_W), wcat, bias_m)


def _head_lanes(g):
    return slice(g * DIFF_DH, (g + 1) * DIFF_DH)


def _diff_lambda(lam_ref, lam_init):
    lp = lam_ref[...]
    return (jnp.exp(jnp.sum(lp[0:1] * lp[1:2], axis=-1, keepdims=True))
            - jnp.exp(jnp.sum(lp[2:3] * lp[3:4], axis=-1, keepdims=True)) + lam_init)


def _diff_diag_mask(tq):
    return (lax.broadcasted_iota(jnp.int32, (tq, 2 * tq), 0) // CHUNK
            <= (lax.broadcasted_iota(jnp.int32, (tq, 2 * tq), 1) % tq) // CHUNK)


def _diff_start(state, g, qi, tq):
    qs_ref, m_ref, l_ref, acc_ref = state
    d_lo = lax.broadcasted_iota(jnp.int32, (DIFF_DH, 2 * tq), 0) < HEAD_DIM
    c_lo = lax.broadcasted_iota(jnp.int32, (DIFF_DH, 2 * tq), 1) < tq
    qt = jnp.concatenate([qi, qi], axis=0).astype(F32).T
    qs_ref[g] = jnp.where(d_lo == c_lo, qt, 0.0).astype(BF16)
    m_ref[g] = jnp.full((1, 2 * tq), NEG_INF, F32)
    l_ref[g] = jnp.zeros((1, 2 * tq), F32)
    acc_ref[g] = jnp.zeros((DIFF_DH, 2 * tq), F32)


def _diff_scores(state, g, kb):
    return _dot(kb, state[0][g])


def _diff_update(state, g, s, vt, mask):
    _, m_ref, l_ref, acc_ref = state
    if mask is not None:
        s = jnp.where(mask, s, NEG_INF)
    m_prev = m_ref[g]
    m_new = jnp.maximum(m_prev, jnp.max(s, axis=0, keepdims=True))
    alpha = jnp.exp2(m_prev - m_new)
    p = jnp.exp2(s - m_new)
    l_ref[g] = alpha * l_ref[g] + jnp.sum(p, axis=0, keepdims=True)
    acc_ref[g] = alpha * acc_ref[g] + _dot(vt, p.astype(BF16))
    m_ref[g] = m_new


def _diff_heads_step(state, n_heads, s_first, key_block, value_block, mask, s_after):
    s_cur, s_nxt = s_first, None
    for g in range(n_heads):
        if g + 1 < n_heads:
            s_nxt = _diff_scores(state, g + 1, key_block(g + 1))
        elif s_after is not None:
            s_nxt = s_after()
        _diff_update(state, g, s_cur, value_block(g), mask)
        s_cur = s_nxt
    return s_nxt


def _diff_finish(state, g, lam, sg, tq, lam_init):
    _, _, l_ref, acc_ref = state
    ot = acc_ref[g] / l_ref[g]
    dt = ot - lam * pltpu.roll(ot, tq, 1)
    dt = dt * lax.rsqrt(jnp.mean(dt * dt, axis=0, keepdims=True) + EPS) * sg
    return ((dt * (1.0 - lam_init)).T[:tq]).astype(BF16)


def _vt(v):
    return v.astype(F32).T.astype(BF16)


def _diff_attn_kernel(lam_ref, sg_ref, q_ref, k_ref, v_ref, o_ref, vt_ref, s_ref, *state, tq, n_heads, lam_init):
    nq = q_ref.shape[1] // tq
    heads = range(n_heads)
    lam = _diff_lambda(lam_ref, lam_init)
    diag_ok = _diff_diag_mask(tq)

    for g in heads:
        for c in range(nq):
            vt_ref[g, c] = _vt(v_ref[0, c * tq:(c + 1) * tq, _head_lanes(g)])

    def key_block(r0):
        return lambda g: k_ref[0, pl.ds(r0, tq), _head_lanes(g)]

    def q_block(i, carry):
        q0 = pl.multiple_of(i * tq, tq)
        for g in heads:
            _diff_start(state, g, q_ref[0, pl.ds(q0, tq), _head_lanes(g)], tq)
        s_ref[...] = _diff_scores(state, 0, key_block(0)(0))

        def past_step(j, c):
            r0 = pl.multiple_of(j * tq, tq)
            r1 = pl.multiple_of(r0 + tq, tq)
            s_ref[...] = _diff_heads_step(state, n_heads, s_ref[...], key_block(r0), lambda g: vt_ref[g, j], None,
                                          lambda: _diff_scores(state, 0, key_block(r1)(0)))
            return c
        lax.fori_loop(0, i, past_step, 0)
        _diff_heads_step(state, n_heads, s_ref[...], key_block(q0), lambda g: vt_ref[g, i], diag_ok, None)
        for g in heads:
            o_ref[0, pl.ds(q0, tq), _head_lanes(g)] = _diff_finish(state, g, lam, sg_ref[...], tq, lam_init)
        return carry

    lax.fori_loop(0, nq, q_block, 0)


def _diff_attn_cached_kernel(lam_ref, sg_ref, q_ref, k_ref, v_ref, ck_ref, cv_ref, o_ref, *state, tq, lam_init):
    c = pl.program_id(1)
    heads = range(DIFF_HEADS)

    @pl.when(c == 0)
    def _():
        for g in heads:
            _diff_start(state, g, q_ref[0, :, _head_lanes(g)], tq)

    cached_keys = lambda g: ck_ref[:, g, :].astype(BF16)
    _diff_heads_step(state, DIFF_HEADS, _diff_scores(state, 0, cached_keys(0)), cached_keys,
                     lambda g: cv_ref[:, g, :].T.astype(BF16), None, None)

    @pl.when(c == pl.num_programs(1) - 1)
    def _():
        lam = _diff_lambda(lam_ref, lam_init)
        new_keys = lambda g: k_ref[0, :, _head_lanes(g)]
        _diff_heads_step(state, DIFF_HEADS, _diff_scores(state, 0, new_keys(0)), new_keys,
                         lambda g: _vt(v_ref[0, :, _head_lanes(g)]), _diff_diag_mask(tq), None)
        for g in heads:
            o_ref[0, :, _head_lanes(g)] = _diff_finish(state, g, lam, sg_ref[...], tq, lam_init)


def _diff_state_scratch(n_heads, tq):
    return [pltpu.VMEM((n_heads, DIFF_DH, 2 * tq), BF16),
            pltpu.VMEM((n_heads, 1, 2 * tq), F32),
            pltpu.VMEM((n_heads, 1, 2 * tq), F32),
            pltpu.VMEM((n_heads, DIFF_DH, 2 * tq), F32)]


def _diff_attn(q, kb, vb, lam_p, subln_g, *, lam_init, name):
    b, s, _ = q.shape
    tq = _row_tile(s, 256)
    n_heads = 4
    grp = pl.BlockSpec((1, s, n_heads * DIFF_DH), lambda bi, hi: (bi, 0, hi))
    return pl.pallas_call(
        functools.partial(_diff_attn_kernel, tq=tq, n_heads=n_heads, lam_init=lam_init),
        grid=(b, DIFF_HEADS // n_heads),
        in_specs=[_resident(lam_p.shape), _resident((DIFF_DH, 1)), grp, grp, grp],
        out_specs=grp,
        out_shape=jax.ShapeDtypeStruct((b, s, DIFF_W), BF16),
        scratch_shapes=[pltpu.VMEM((n_heads, s // tq, DIFF_DH, tq), BF16), pltpu.VMEM((tq, 2 * tq), F32)]
        + _diff_state_scratch(n_heads, tq),
        compiler_params=pltpu.CompilerParams(dimension_semantics=("arbitrary", "arbitrary"),
                                             vmem_limit_bytes=V7X_VMEM_LIMIT_BYTES),
        name=name,
    )(lam_p, subln_g.reshape(DIFF_DH, 1), q, kb, vb)


def _diff_attn_cached(q, kb, vb, cache_k, cache_v, layer_j, lam_p, subln_g, *, lam_init, name):
    b, s, _ = q.shape
    assert s % CHUNK == 0 and s <= 256, s
    lc = cache_k.shape[2]
    tkc = _row_tile(lc, 512)
    new = pl.BlockSpec((1, s, DIFF_W), lambda bi, ci: (bi, 0, 0))
    cache = pl.BlockSpec((None, None, tkc, DIFF_HEADS, DIFF_DH), lambda bi, ci: (layer_j, bi, ci, 0, 0))
    return pl.pallas_call(
        functools.partial(_diff_attn_cached_kernel, tq=s, lam_init=lam_init),
        grid=(b, lc // tkc),
        in_specs=[_resident(lam_p.shape), _resident((DIFF_DH, 1)), new, new, new, cache, cache],
        out_specs=new,
        out_shape=jax.ShapeDtypeStruct((b, s, DIFF_W), BF16),
        scratch_shapes=_diff_state_scratch(DIFF_HEADS, s),
        compiler_params=pltpu.CompilerParams(dimension_semantics=("arbitrary", "arbitrary"),
                                             vmem_limit_bytes=V7X_VMEM_LIMIT_BYTES),
        name=name,
    )(lam_p, subln_g.reshape(DIFF_DH, 1), q, kb, vb, cache_k, cache_v)


def _block_diag(w):
    g, a, bb = w.shape
    eye = jnp.eye(g, dtype=w.dtype)
    return (eye[:, None, :, None] * w[:, :, None, :]).reshape(g * a, g * bb)


def _trunk(x, prompt, caches, wts, tag):
    cache_pool, cache_swa_k, cache_swa_v, cache_diff_k, cache_diff_v = caches
    b, s, d = x.shape
    depth = wts["norm_g"].shape[0]
    norm_g = wts["norm_g"]
    xt = x.reshape(b * s, d)
    pool_s, swa_k, swa_v, diff_k, diff_v, gmlp_v = [], [], [], [], [], []
    pending_mix = None
    past_len = 0 if prompt else cache_diff_k.shape[2]

    for l in range(depth):
        j = l // 2
        f0, f1 = wts["ffn"][l]
        xt = _ffn(xt, norm_g[l, 0], *f0, mix=pending_mix, name=f"{tag}_ffn{l}a")
        pending_mix = None
        if l % 2 == 0:
            if prompt:
                cpool = jnp.zeros((b, POOL_HALO, POOL_W), F32)
                ckv = jnp.zeros((b, SWA_WINDOW, 2 * SWA_KV_W), F32)
            else:
                cpool = jnp.pad(cache_pool[j], ((0, 0), (POOL_HALO - cache_pool.shape[2], 0), (0, 0)))
                ckv = jnp.concatenate([cache_swa_k[j].reshape(b, SWA_WINDOW, SWA_KV_W),
                                       cache_swa_v[j].reshape(b, SWA_WINDOW, SWA_KV_W)], axis=-1)
            x3, pool_o, kv_o = _even_mixer(
                xt.reshape(b, s, d), norm_g[l, 1], wts["even_w_in"][j], wts["pool_wbd"][j],
                wts["pool_scale"][j], wts["swa_sink"][j], wts["even_w_out"][j], cpool, ckv,
                pos0=past_len, has_cache=not prompt, name=f"{tag}_even{l}")
            xt = x3.reshape(b * s, d)
            pool_s.append(pool_o[:, 1:])
            swa_k.append(kv_o[..., :SWA_KV_W].reshape(b, SWA_WINDOW, 2, HEAD_DIM))
            swa_v.append(kv_o[..., SWA_KV_W:].reshape(b, SWA_WINDOW, 2, HEAD_DIM))
            xt = _ffn(xt, norm_g[l, 2], *f1, name=f"{tag}_ffn{l}b")
        else:
            lc = 128 if prompt else s
            outs = _odd_proj(xt, norm_g[l, 1], wts["odd_w_in"][j], wts["gmlp_ln_g"][j], wts["gmlp_ln_b"][j],
                             wts["gmlp_wcat"][(j, lc)], wts["gmlp_bias"][(j, lc)],
                             lc=lc, with_vg=not prompt, name=f"{tag}_oddproj{l}")
            q, k, v, kb, vb, dd = outs[:6]
            diff_k.append(k.reshape(b, s, DIFF_HEADS, DIFF_DH))
            diff_v.append(v.reshape(b, s, DIFF_HEADS, DIFF_DH))
            if not prompt:
                gmlp_v.append(outs[6].reshape(b, s, GMLP_W))
            lam_init = 0.8 - 0.6 * math.exp(-0.3 * l)
            qkv = (q.reshape(b, s, DIFF_W), kb.reshape(b, s, DIFF_W), vb.reshape(b, s, DIFF_W))
            if prompt:
                o = _diff_attn(*qkv, wts["diff_lambda"][j], wts["diff_subln_g"][j],
                               lam_init=lam_init, name=f"{tag}_diffattn{l}")
            else:
                o = _diff_attn_cached(*qkv, cache_diff_k, cache_diff_v, j, wts["diff_lambda"][j],
                                      wts["diff_subln_g"][j], lam_init=lam_init, name=f"{tag}_diffattn{l}")
            w_out = wts["odd_w_out"][j]
            mix = (o.reshape(b * s, DIFF_W), dd, w_out[:DIFF_W], w_out[DIFF_W:])
            last = l == depth - 1
            xt = _ffn(xt, norm_g[l, 2], *f1, mix=mix, final_g=wts["final_g"] if last else None,
                      name=f"{tag}_ffn{l}b")
            pending_mix = None
    if depth % 2 == 1:
        raise NotImplementedError("odd depth")
    return xt.reshape(b, s, d), pool_s, swa_k, swa_v, diff_k, diff_v, gmlp_v


def _prep_weights(norm_g, final_g, ffn_gate, ffn_up, ffn_down, even_w_in, even_w_out, pool_w, pool_scale,
                  swa_sink, odd_w_in, odd_w_out, diff_lambda, diff_subln_g, gmlp_ln_g, gmlp_ln_b,
                  gmlp_w_s, gmlp_b_s, gmlp_chunks):
    depth = norm_g.shape[0]
    ffn = [[(ffn_gate[l, i].astype(BF16), ffn_up[l, i].astype(BF16), ffn_down[l, i].astype(BF16))
            for i in range(2)] for l in range(depth)]
    n_even, n_odd = even_w_in.shape[0], odd_w_in.shape[0]
    wcat, bias = {}, {}
    for j in range(n_odd):
        for lc in gmlp_chunks:
            ws = gmlp_w_s[j][:, :lc, :lc]
            wcat[(j, lc)] = jnp.transpose(ws, (1, 0, 2)).reshape(lc, GMLP_GROUPS * lc)
            bias[(j, lc)] = jnp.repeat(gmlp_b_s[j][:, :lc].T, GMLP_GW, axis=1)
    return dict(
        norm_g=norm_g, final_g=final_g, ffn=ffn,
        even_w_in=[even_w_in[j].astype(BF16) for j in range(n_even)],
        even_w_out=[even_w_out[j].astype(BF16) for j in range(n_even)],
        pool_wbd=[_block_diag(pool_w[j]).astype(BF16) for j in range(n_even)],
        pool_scale=pool_scale, swa_sink=swa_sink,
        odd_w_in=[odd_w_in[j].astype(BF16) for j in range(n_odd)],
        odd_w_out=[odd_w_out[j].astype(BF16) for j in range(n_odd)],
        diff_lambda=diff_lambda, diff_subln_g=diff_subln_g, gmlp_ln_g=gmlp_ln_g, gmlp_ln_b=gmlp_ln_b,
        gmlp_wcat=wcat, gmlp_bias=bias,
    )


def kernel(x_prompt, x_sample, cache_pool, cache_swa_k, cache_swa_v, cache_diff_k, cache_diff_v, norm_g, final_g, ffn_gate, ffn_up, ffn_down, even_w_in, even_w_out, pool_w, pool_scale, swa_sink, odd_w_in, odd_w_out, diff_lambda, diff_subln_g, gmlp_ln_g, gmlp_ln_b, gmlp_w_s, gmlp_b_s):
    wts = _prep_weights(norm_g, final_g, ffn_gate, ffn_up, ffn_down, even_w_in, even_w_out, pool_w,
                        pool_scale, swa_sink, odd_w_in, odd_w_out, diff_lambda, diff_subln_g, gmlp_ln_g,
                        gmlp_ln_b, gmlp_w_s, gmlp_b_s, gmlp_chunks=(128, x_sample.shape[1]))
    caches = (cache_pool, cache_swa_k, cache_swa_v, cache_diff_k, cache_diff_v)
    y_p, pool_p, swa_k_p, swa_v_p, diff_k_p, diff_v_p, _ = _trunk(x_prompt, True, caches, wts, "p")
    y_s, pool_s, swa_k_s, swa_v_s, diff_k_s, diff_v_s, gmlp_v_s = _trunk(x_sample, False, caches, wts, "s")
    return (y_p, y_s,
            jnp.stack(pool_p), jnp.stack(pool_s),
            jnp.stack(swa_k_p), jnp.stack(swa_k_s),
            jnp.stack(swa_v_p), jnp.stack(swa_v_s),
            jnp.stack(diff_k_p), jnp.stack(diff_k_s),
            jnp.stack(diff_v_p), jnp.stack(diff_v_s),
            jnp.stack(gmlp_v_s))
```

```python
import functools
import math

import jax
import jax.numpy as jnp
from jax import lax
from jax.experimental import pallas as pl
from jax.experimental.pallas import tpu as pltpu

F32 = jnp.float32
BF16 = jnp.bfloat16

EPS = 1e-6
CHUNK = 64
HEAD_DIM = 64
POOL_WINDOWS = (2, 4, 8, 16)
POOL_GW = 96
POOL_W = 4 * POOL_GW
POOL_HALO = 16
SWA_WINDOW = 128
SWA_KEYS = 256
SWA_HEADS = 16
SWA_Q_W = SWA_HEADS * HEAD_DIM
SWA_KV_W = 2 * HEAD_DIM
DIFF_HEADS = 8
DIFF_DH = 2 * HEAD_DIM
DIFF_W = DIFF_HEADS * DIFF_DH
GMLP_GROUPS = 4
GMLP_GW = 96
GMLP_W = GMLP_GROUPS * GMLP_GW
SM_SCALE = HEAD_DIM ** -0.5
LOG2_E = math.log2(math.e)

V7X_VMEM_LIMIT_BYTES = 56 * 1024 * 1024
NEG_INF = float("-inf")


def _rms(x, g):
    return x * lax.rsqrt(jnp.mean(x * x, axis=-1, keepdims=True) + EPS) * g


def _dot(a, b):
    return jnp.dot(a, b, preferred_element_type=F32)


def _dot_nt(a, b):
    return lax.dot_general(a, b, (((1,), (1,)), ((), ())), preferred_element_type=F32)


def _resident(shape):
    nd = len(shape)
    return pl.BlockSpec(shape, lambda *_: (0,) * nd, pipeline_mode=pl.Buffered(1))


def _row_tile(n_rows, want):
    t = min(want, n_rows)
    assert n_rows % t == 0, (n_rows, t)
    return t


def _ffn_kernel(*refs, ff_chunk, has_mix, has_final):
    it = iter(refs)
    x_ref = next(it)
    if has_mix:
        o_ref, d_ref, woa_ref, wob_ref = next(it), next(it), next(it), next(it)
    g_ref, wg_ref, wu_ref, wd_ref = next(it), next(it), next(it), next(it)
    if has_final:
        fg_ref = next(it)
    out_ref = next(it)
    acc_ref = next(it)

    x = x_ref[...]
    if has_mix:
        x = x + _dot(o_ref[...], woa_ref[...]) + _dot(d_ref[...], wob_ref[...])
    hb = _rms(x, g_ref[...]).astype(BF16)
    d_ff = wg_ref.shape[1]
    for c in range(d_ff // ff_chunk):
        sl = slice(c * ff_chunk, (c + 1) * ff_chunk)
        gate = _dot(hb, wg_ref[:, sl])
        up = _dot(hb, wu_ref[:, sl])
        act = (gate * jax.nn.sigmoid(gate) * up).astype(BF16)
        part = _dot(act, wd_ref[sl, :])
        if c == 0:
            acc_ref[...] = part
        else:
            acc_ref[...] += part
    y = x + 0.5 * acc_ref[...]
    if has_final:
        y = _rms(y, fg_ref[...])
    out_ref[...] = y


def _ffn(x, g, wg, wu, wd, mix=None, final_g=None, *, name):
    t, d = x.shape
    d_ff = wg.shape[1]
    tm = _row_tile(t, 512)
    ff_chunk = 256
    assert d_ff % ff_chunk == 0
    row = lambda w: pl.BlockSpec((tm, w), lambda i: (i, 0))
    args, specs = [x], [row(d)]
    if mix is not None:
        o, dd, woa, wob = mix
        args += [o, dd, woa, wob]
        specs += [row(o.shape[1]), row(dd.shape[1]), _resident(woa.shape), _resident(wob.shape)]
    args += [g.reshape(1, d), wg, wu, wd]
    specs += [_resident((1, d)), _resident(wg.shape), _resident(wu.shape), _resident(wd.shape)]
    if final_g is not None:
        args.append(final_g.reshape(1, d))
        specs.append(_resident((1, d)))
    kern = functools.partial(_ffn_kernel, ff_chunk=ff_chunk, has_mix=mix is not None,
                             has_final=final_g is not None)
    return pl.pallas_call(
        kern,
        grid=(t // tm,),
        in_specs=specs,
        out_specs=row(d),
        out_shape=jax.ShapeDtypeStruct((t, d), F32),
        scratch_shapes=[pltpu.VMEM((tm, d), F32)],
        compiler_params=pltpu.CompilerParams(dimension_semantics=("arbitrary",),
                                             vmem_limit_bytes=V7X_VMEM_LIMIT_BYTES),
        name=name,
    )(*args)


def _even_kernel(sink_ref, x_ref, g_ref, win_ref, pw_ref, ps_ref, wout_ref, cpool_ref, ckv_ref,
                 xo_ref, pool_o_ref, kv_o_ref,
                 e_ref, kv_ref, qb_ref, bo_ref, *, ts, tq, pos0, has_cache):
    sb = pl.program_id(1)
    n_sub = ts // tq
    kv_rows = kv_ref.shape[0]

    @pl.when(sb == 0)
    def _():
        e_ref[0:POOL_HALO, :] = cpool_ref[0]
        kv_ref[0:SWA_WINDOW, :] = ckv_ref[0]
        if kv_rows > SWA_WINDOW + ts:
            kv_ref[SWA_WINDOW + ts:, :] = jnp.zeros((kv_rows - SWA_WINDOW - ts, 2 * SWA_KV_W), F32)

    x = x_ref[0]
    hb = _rms(x, g_ref[...]).astype(BF16)
    u = _dot(hb, win_ref[:, 0:POOL_W])
    qb_ref[...] = (_dot(hb, win_ref[:, POOL_W:POOL_W + SWA_Q_W]) * (SM_SCALE * LOG2_E)).astype(BF16)
    kv_ref[SWA_WINDOW:SWA_WINDOW + ts, :] = _dot(hb, win_ref[:, POOL_W + SWA_Q_W:])
    e_ref[POOL_HALO:POOL_HALO + ts, :] = u

    ev = e_ref[...]
    s2 = ev + pltpu.roll(ev, 1, 0)
    s4 = s2 + pltpu.roll(s2, 2, 0)
    s8 = s4 + pltpu.roll(s4, 4, 0)
    s16 = s8 + pltpu.roll(s8, 8, 0)
    lane = lax.broadcasted_iota(jnp.int32, (1, POOL_W), 1)
    g0, g1, g2 = lane < POOL_GW, lane < 2 * POOL_GW, lane < 3 * POOL_GW
    sw = jnp.where(g0, s2, jnp.where(g1, s4, jnp.where(g2, s8, s16)))[POOL_HALO:]
    win = jnp.where(g0, POOL_WINDOWS[0], jnp.where(g1, POOL_WINDOWS[1],
                                                  jnp.where(g2, POOL_WINDOWS[2], POOL_WINDOWS[3])))
    pos = pos0 + sb * ts + lax.broadcasted_iota(jnp.int32, (ts, 1), 0)
    cnt = jnp.minimum(pos + 1, win).astype(F32)
    diff = sw / cnt - u
    a_out = _dot(diff.astype(BF16), pw_ref[...]) * ps_ref[...]
    new_halo = ev[ts:ts + POOL_HALO]
    e_ref[0:POOL_HALO, :] = new_halo
    pool_o_ref[0] = new_halo

    lane128 = lax.broadcasted_iota(jnp.int32, (1, 2 * HEAD_DIM), 1)
    lo = lane128 < HEAD_DIM
    rq = lax.broadcasted_iota(jnp.int32, (tq, 2 * SWA_KEYS), 0) // CHUNK
    col = lax.broadcasted_iota(jnp.int32, (tq, 2 * SWA_KEYS), 1)
    ek = jnp.where(col >= SWA_KEYS, col - SWA_KEYS, col)
    ekc = ek // CHUNK
    band = (rq <= ekc) & (ekc <= rq + SWA_WINDOW // CHUNK)

    def halves(xa, xb):
        return jnp.concatenate([jnp.where(lo, xa, 0.0), jnp.where(lo, 0.0, xb)], axis=0).astype(BF16)

    row_lo = lax.broadcasted_iota(jnp.int32, (2 * HEAD_DIM, 1), 0) < HEAD_DIM

    def halves_t(xa, xb):
        return jnp.concatenate([jnp.where(row_lo, xa, 0.0), jnp.where(row_lo, 0.0, xb)], axis=1).astype(BF16)

    def sub_block(a, carry):
        r0 = pl.multiple_of(a * tq, tq)
        kve = kv_ref[pl.ds(r0, SWA_KEYS), :]
        ke, ve = kve[:, :SWA_KV_W], kve[:, SWA_KV_W:]
        ve_r = pltpu.roll(ve, HEAD_DIM, 1)
        v2 = (halves(ve, ve_r), halves(ve_r, ve))
        ket = ke.T
        ket_r = pltpu.roll(ket, HEAD_DIM, 0)
        k2t = (halves_t(ket, ket_r), halves_t(ket_r, ket))
        if has_cache:
            valid = band
        else:
            valid = band & ((ek >= SWA_WINDOW) | (sb * n_sub + a > 0))
        n_pairs = SWA_HEADS // 2

        def scores(j):
            return _dot(qb_ref[pl.ds(r0, tq), 128 * j:128 * (j + 1)], k2t[(2 * j) // n_pairs])

        s_cur = scores(0)
        for j in range(n_pairs):
            s_nxt = scores(j + 1) if j + 1 < n_pairs else None
            s = jnp.where(valid, s_cur, NEG_INF)
            ps, dens = [], []
            for half in range(2):
                sh = s[:, half * SWA_KEYS:(half + 1) * SWA_KEYS]
                sink = sink_ref[2 * j + half] * LOG2_E
                m = jnp.maximum(jnp.max(sh, axis=-1, keepdims=True), sink)
                p = jnp.exp2(sh - m)
                ps.append(p)
                dens.append(jnp.sum(p, axis=-1, keepdims=True) + jnp.exp2(sink - m))
            p = jnp.concatenate(ps, axis=1).astype(BF16)
            o = _dot(p, v2[(2 * j) // n_pairs])
            o = o / jnp.where(lo, dens[0], dens[1])
            bo_ref[pl.ds(r0, tq), 128 * j:128 * (j + 1)] = o.astype(BF16)
            s_cur = s_nxt
        return carry

    lax.fori_loop(0, n_sub, sub_block, 0)

    y = _dot(a_out.astype(BF16), wout_ref[0:POOL_W, :]) + _dot(bo_ref[...], wout_ref[POOL_W:, :])
    xo_ref[0] = x + y

    new_win = kv_ref[ts:ts + SWA_WINDOW, :]
    kv_ref[0:SWA_WINDOW, :] = new_win
    kv_o_ref[0] = new_win


def _even_mixer(x, g, w_in, pool_wbd, pool_scale, sink, w_out, cache_pool, cache_kv, *, pos0, has_cache, name):
    b, s, d = x.shape
    ts = _row_tile(s, 512)
    tq = min(ts, SWA_WINDOW)
    kv_rows = max(SWA_WINDOW + ts, (ts // tq - 1) * tq + SWA_KEYS)
    kern = functools.partial(_even_kernel, ts=ts, tq=tq, pos0=pos0, has_cache=has_cache)
    per_b = lambda r, w: pl.BlockSpec((1, r, w), lambda bi, si: (bi, 0, 0))
    return pl.pallas_call(
        kern,
        grid=(b, s // ts),
        in_specs=[
            pl.BlockSpec(memory_space=pltpu.SMEM),
            pl.BlockSpec((1, ts, d), lambda bi, si: (bi, si, 0)),
            _resident((1, d)), _resident(w_in.shape), _resident(pool_wbd.shape), _resident((1, POOL_W)),
            _resident(w_out.shape),
            per_b(POOL_HALO, POOL_W), per_b(SWA_WINDOW, 2 * SWA_KV_W),
        ],
        out_specs=[
            pl.BlockSpec((1, ts, d), lambda bi, si: (bi, si, 0)),
            per_b(POOL_HALO, POOL_W), per_b(SWA_WINDOW, 2 * SWA_KV_W),
        ],
        out_shape=[
            jax.ShapeDtypeStruct((b, s, d), F32),
            jax.ShapeDtypeStruct((b, POOL_HALO, POOL_W), F32),
            jax.ShapeDtypeStruct((b, SWA_WINDOW, 2 * SWA_KV_W), F32),
        ],
        scratch_shapes=[
            pltpu.VMEM((POOL_HALO + ts, POOL_W), F32),
            pltpu.VMEM((kv_rows, 2 * SWA_KV_W), F32),
            pltpu.VMEM((ts, SWA_Q_W), BF16),
            pltpu.VMEM((ts, SWA_Q_W), BF16),
        ],
        compiler_params=pltpu.CompilerParams(dimension_semantics=("arbitrary", "arbitrary"),
                                             vmem_limit_bytes=V7X_VMEM_LIMIT_BYTES),
        name=name,
    )(sink, x, g.reshape(1, d), w_in, pool_wbd, pool_scale.reshape(1, POOL_W), w_out, cache_pool, cache_kv)


def _gelu(x):
    return 0.5 * x * (1.0 + lax.erf(x * math.sqrt(0.5)))


def _odd_proj_kernel(*refs, lc, with_vg):
    (x_ref, g_ref, w_ref, lng_ref, lnb_ref, wcat_ref, bm_ref,
     q_ref, k_ref, v_ref, kb_ref, vb_ref, d_ref) = refs[:13]
    vg_ref = refs[13] if with_vg else None
    tm = x_ref.shape[0]

    hb = _rms(x_ref[...], g_ref[...]).astype(BF16)
    q_ref[...] = (_dot(hb, w_ref[:, 0:DIFF_W]) * (SM_SCALE * LOG2_E)).astype(BF16)
    k = _dot(hb, w_ref[:, DIFF_W:2 * DIFF_W])
    k_ref[...] = k
    kb_ref[...] = k.astype(BF16)
    v = _dot(hb, w_ref[:, 2 * DIFF_W:3 * DIFF_W])
    v_ref[...] = v
    vb_ref[...] = v.astype(BF16)

    u = _gelu(_dot(hb, w_ref[:, 3 * DIFF_W:3 * DIFF_W + GMLP_W]))
    gv = _gelu(_dot(hb, w_ref[:, 3 * DIFF_W + GMLP_W:]))
    mu = jnp.mean(gv, axis=-1, keepdims=True)
    cen = gv - mu
    var = jnp.mean(cen * cen, axis=-1, keepdims=True)
    vg = cen * lax.rsqrt(var + EPS) * lng_ref[...] + lnb_ref[...]
    if with_vg:
        vg_ref[...] = vg

    t_i = lax.broadcasted_iota(jnp.int32, (lc, GMLP_GROUPS * lc), 0)
    s_i = lax.broadcasted_iota(jnp.int32, (lc, GMLP_GROUPS * lc), 1) % lc
    wcat = jnp.where(s_i <= t_i, wcat_ref[...], 0.0).astype(BF16)
    lane = lax.broadcasted_iota(jnp.int32, (1, GMLP_W), 1) // GMLP_GW
    for c in range(tm // lc):
        rows = slice(c * lc, (c + 1) * lc)
        vc = vg[rows]
        vstack = jnp.concatenate([jnp.where(lane == gi, vc, 0.0) for gi in range(GMLP_GROUPS)],
                                 axis=0).astype(BF16)
        mix = _dot(wcat, vstack) + bm_ref[...]
        d_ref[rows, :] = (u[rows] * mix).astype(BF16)


def _odd_proj(x, g, w_in, ln_g, ln_b, wcat, bias_m, *, lc, with_vg, name):
    t, d = x.shape
    tm = _row_tile(t, 512)
    assert tm % lc == 0
    row = lambda w: pl.BlockSpec((tm, w), lambda i: (i, 0))
    out_shape = [
        jax.ShapeDtypeStruct((t, DIFF_W), BF16),
        jax.ShapeDtypeStruct((t, DIFF_W), F32),
        jax.ShapeDtypeStruct((t, DIFF_W), F32),
        jax.ShapeDtypeStruct((t, DIFF_W), BF16),
        jax.ShapeDtypeStruct((t, DIFF_W), BF16),
        jax.ShapeDtypeStruct((t, GMLP_W), BF16),
    ]
    out_specs = [row(DIFF_W)] * 5 + [row(GMLP_W)]
    if with_vg:
        out_shape.append(jax.ShapeDtypeStruct((t, GMLP_W), F32))
        out_specs.append(row(GMLP_W))
    return pl.pallas_call(
        functools.partial(_odd_proj_kernel, lc=lc, with_vg=with_vg),
        grid=(t // tm,),
        in_specs=[row(d), _resident((1, d)), _resident(w_in.shape), _resident((1, GMLP_W)),
                  _resident((1, GMLP_W)), _resident(wcat.shape), _resident(bias_m.shape)],
        out_specs=out_specs,
        out_shape=out_shape,
        compiler_params=pltpu.CompilerParams(dimension_semantics=("arbitrary",),
                                             vmem_limit_bytes=V7X_VMEM_LIMIT_BYTES),
        name=name,
    )(x, g.reshape(1, d), w_in, ln_g.reshape(1, GMLP_W), ln_b.reshape(1, GMLP_W), wcat, bias_m)


def _head_lanes(g):
    return slice(g * DIFF_DH, (g + 1) * DIFF_DH)


def _diff_lambda(lam_ref, lam_init):
    lp = lam_ref[...]
    return (jnp.exp(jnp.sum(lp[0:1] * lp[1:2], axis=-1, keepdims=True))
            - jnp.exp(jnp.sum(lp[2:3] * lp[3:4], axis=-1, keepdims=True)) + lam_init)


def _diff_diag_mask(tq):
    return (lax.broadcasted_iota(jnp.int32, (tq, 2 * tq), 0) // CHUNK
            <= (lax.broadcasted_iota(jnp.int32, (tq, 2 * tq), 1) % tq) // CHUNK)


def _diff_start(state, g, qi, tq):
    qs_ref, m_ref, l_ref, acc_ref = state
    d_lo = lax.broadcasted_iota(jnp.int32, (DIFF_DH, 2 * tq), 0) < HEAD_DIM
    c_lo = lax.broadcasted_iota(jnp.int32, (DIFF_DH, 2 * tq), 1) < tq
    qt = jnp.concatenate([qi, qi], axis=0).astype(F32).T
    qs_ref[g] = jnp.where(d_lo == c_lo, qt, 0.0).astype(BF16)
    if m_ref is not None:
        m_ref[g] = jnp.full((1, 2 * tq), NEG_INF, F32)
    l_ref[g] = jnp.zeros((1, 2 * tq), F32)
    acc_ref[g] = jnp.zeros((DIFF_DH, 2 * tq), F32)


def _diff_scores(state, g, kb):
    return _dot(kb, state[0][g])


def _diff_update(state, g, s, vt, mask):
    _, m_ref, l_ref, acc_ref = state
    if mask is not None:
        s = jnp.where(mask, s, NEG_INF)
    m_prev = m_ref[g]
    m_new = jnp.maximum(m_prev, jnp.max(s, axis=0, keepdims=True))
    alpha = jnp.exp2(m_prev - m_new)
    p = jnp.exp2(s - m_new)
    l_ref[g] = alpha * l_ref[g] + jnp.sum(p, axis=0, keepdims=True)
    acc_ref[g] = alpha * acc_ref[g] + _dot(vt, p.astype(BF16))
    m_ref[g] = m_new


def _diff_heads_step(state, n_heads, s_first, key_block, value_block, mask, s_after):
    s_cur, s_nxt = s_first, None
    for g in range(n_heads):
        if g + 1 < n_heads:
            s_nxt = _diff_scores(state, g + 1, key_block(g + 1))
        elif s_after is not None:
            s_nxt = s_after()
        _diff_update(state, g, s_cur, value_block(g), mask)
        s_cur = s_nxt
    return s_nxt


def _diff_finish(state, g, lam, sg, tq, lam_init):
    _, _, l_ref, acc_ref = state
    ot = acc_ref[g] / l_ref[g]
    dt = ot - lam * pltpu.roll(ot, tq, 1)
    dt = dt * lax.rsqrt(jnp.mean(dt * dt, axis=0, keepdims=True) + EPS) * sg
    return ((dt * (1.0 - lam_init)).T[:tq]).astype(BF16)


def _vt(v):
    return v.astype(F32).T.astype(BF16)


def _diff_attn_kernel(lam_ref, sg_ref, q_ref, k_ref, v_ref, o_ref, vt_ref, s_ref, m_ref, a_ref,
                      qs_ref, l_ref, acc_ref, *, tq, n_heads, lam_init):
    nq = q_ref.shape[1] // tq
    heads = range(n_heads)
    state = (qs_ref, None, l_ref, acc_ref)
    lam = _diff_lambda(lam_ref, lam_init)
    diag_ok = _diff_diag_mask(tq)

    for g in heads:
        for c in range(nq):
            vt_ref[g, c] = _vt(v_ref[0, c * tq:(c + 1) * tq, _head_lanes(g)])

    def logits_phase(g, r0, src, dst, mask):
        s = _dot(k_ref[0, pl.ds(r0, tq), _head_lanes(g)], qs_ref[g])
        if mask is not None:
            s = jnp.where(mask, s, NEG_INF)
        s_ref[dst, g] = s
        m_old = m_ref[src, g]
        m_new = jnp.maximum(m_old, jnp.max(s, axis=0, keepdims=True))
        m_ref[dst, g] = m_new
        a_ref[dst, g] = jnp.exp2(m_old - m_new)

    def accum_phase(g, blk, slot):
        p = jnp.exp2(s_ref[slot, g] - m_ref[slot, g])
        alpha = a_ref[slot, g]
        l_ref[g] = alpha * l_ref[g] + jnp.sum(p, axis=0, keepdims=True)
        acc_ref[g] = alpha * acc_ref[g] + _dot(vt_ref[g, blk], p.astype(BF16))

    def step(blk, slot, mask):
        r1 = pl.multiple_of((blk + 1) * tq, tq)
        logits_phase(0, r1, slot, 1 - slot, mask)
        for g in heads:
            if g + 1 < n_heads:
                logits_phase(g + 1, r1, slot, 1 - slot, mask)
            accum_phase(g, blk, slot)

    def q_block(i, carry):
        q0 = pl.multiple_of(i * tq, tq)
        for g in heads:
            _diff_start(state, g, q_ref[0, pl.ds(q0, tq), _head_lanes(g)], tq)
            m_ref[1, g] = jnp.full((1, 2 * tq), NEG_INF, F32)

        @pl.when(i == 0)
        def _():
            for g in heads:
                logits_phase(g, 0, 1, 0, diag_ok)

        @pl.when(i > 0)
        def _():
            for g in heads:
                logits_phase(g, 0, 1, 0, None)

        n_plain = jnp.maximum(i - 1, 0)

        def two_steps(t, c):
            step(2 * t, 0, None)
            step(2 * t + 1, 1, None)
            return c
        lax.fori_loop(0, n_plain // 2, two_steps, 0)

        @pl.when(n_plain % 2 == 1)
        def _():
            step(n_plain - 1, 0, None)

        @pl.when(i == 0)
        def _():
            for g in heads:
                accum_phase(g, i, 0)

        @pl.when(i % 2 == 1)
        def _():
            step(i - 1, 0, diag_ok)
            for g in heads:
                accum_phase(g, i, 1)

        @pl.when((i % 2 == 0) & (i > 0))
        def _():
            step(i - 1, 1, diag_ok)
            for g in heads:
                accum_phase(g, i, 0)

        for g in heads:
            o_ref[0, pl.ds(q0, tq), _head_lanes(g)] = _diff_finish(state, g, lam, sg_ref[...], tq, lam_init)
        return carry

    lax.fori_loop(0, nq, q_block, 0)


def _diff_attn_cached_kernel(lam_ref, sg_ref, q_ref, k_ref, v_ref, ck_ref, cv_ref, o_ref, *state, tq, lam_init):
    c = pl.program_id(1)
    heads = range(DIFF_HEADS)

    @pl.when(c == 0)
    def _():
        for g in heads:
            _diff_start(state, g, q_ref[0, :, _head_lanes(g)], tq)

    tkc = ck_ref.shape[0] // DIFF_HEADS
    head_rows = lambda g: pl.ds(g, tkc, stride=DIFF_HEADS)
    cached_keys = lambda g: ck_ref[head_rows(g), :].astype(BF16)
    _diff_heads_step(state, DIFF_HEADS, _diff_scores(state, 0, cached_keys(0)), cached_keys,
                     lambda g: cv_ref[head_rows(g), :].T.astype(BF16), None, None)

    @pl.when(c == pl.num_programs(1) - 1)
    def _():
        lam = _diff_lambda(lam_ref, lam_init)
        new_keys = lambda g: k_ref[0, :, _head_lanes(g)]
        _diff_heads_step(state, DIFF_HEADS, _diff_scores(state, 0, new_keys(0)), new_keys,
                         lambda g: _vt(v_ref[0, :, _head_lanes(g)]), _diff_diag_mask(tq), None)
        for g in heads:
            o_ref[0, :, _head_lanes(g)] = _diff_finish(state, g, lam, sg_ref[...], tq, lam_init)


def _diff_state_scratch(n_heads, tq):
    return [pltpu.VMEM((n_heads, DIFF_DH, 2 * tq), BF16),
            pltpu.VMEM((n_heads, 1, 2 * tq), F32),
            pltpu.VMEM((n_heads, 1, 2 * tq), F32),
            pltpu.VMEM((n_heads, DIFF_DH, 2 * tq), F32)]


def _diff_attn(q, kb, vb, lam_p, subln_g, *, lam_init, name):
    b, s, _ = q.shape
    tq = _row_tile(s, 256)
    n_heads = 4
    grp = pl.BlockSpec((1, s, n_heads * DIFF_DH), lambda bi, hi: (bi, 0, hi))
    return pl.pallas_call(
        functools.partial(_diff_attn_kernel, tq=tq, n_heads=n_heads, lam_init=lam_init),
        grid=(b, DIFF_HEADS // n_heads),
        in_specs=[_resident(lam_p.shape), _resident((DIFF_DH, 1)), grp, grp, grp],
        out_specs=grp,
        out_shape=jax.ShapeDtypeStruct((b, s, DIFF_W), BF16),
        scratch_shapes=[pltpu.VMEM((n_heads, s // tq, DIFF_DH, tq), BF16),
                        pltpu.VMEM((2, n_heads, tq, 2 * tq), F32),
                        pltpu.VMEM((2, n_heads, 1, 2 * tq), F32),
                        pltpu.VMEM((2, n_heads, 1, 2 * tq), F32),
                        pltpu.VMEM((n_heads, DIFF_DH, 2 * tq), BF16),
                        pltpu.VMEM((n_heads, 1, 2 * tq), F32),
                        pltpu.VMEM((n_heads, DIFF_DH, 2 * tq), F32)],
        compiler_params=pltpu.CompilerParams(dimension_semantics=("arbitrary", "arbitrary"),
                                             vmem_limit_bytes=V7X_VMEM_LIMIT_BYTES),
        name=name,
    )(lam_p, subln_g.reshape(DIFF_DH, 1), q, kb, vb)


def _diff_attn_cached(q, kb, vb, cache_k, cache_v, layer_j, lam_p, subln_g, *, lam_init, name):
    b, s, _ = q.shape
    assert s % CHUNK == 0 and s <= 256, s
    n_layers, _, lc = cache_k.shape[:3]
    tkc = _row_tile(lc, 512)
    cache_k, cache_v = (c.reshape(n_layers, b, lc * DIFF_HEADS, DIFF_DH) for c in (cache_k, cache_v))
    new = pl.BlockSpec((1, s, DIFF_W), lambda bi, ci: (bi, 0, 0))
    cache = pl.BlockSpec((None, None, tkc * DIFF_HEADS, DIFF_DH), lambda bi, ci: (layer_j, bi, ci, 0))
    return pl.pallas_call(
        functools.partial(_diff_attn_cached_kernel, tq=s, lam_init=lam_init),
        grid=(b, lc // tkc),
        in_specs=[_resident(lam_p.shape), _resident((DIFF_DH, 1)), new, new, new, cache, cache],
        out_specs=new,
        out_shape=jax.ShapeDtypeStruct((b, s, DIFF_W), BF16),
        scratch_shapes=_diff_state_scratch(DIFF_HEADS, s),
        compiler_params=pltpu.CompilerParams(dimension_semantics=("arbitrary", "arbitrary"),
                                             vmem_limit_bytes=V7X_VMEM_LIMIT_BYTES),
        name=name,
    )(lam_p, subln_g.reshape(DIFF_DH, 1), q, kb, vb, cache_k, cache_v)


def _block_diag(w):
    g, a, bb = w.shape
    eye = jnp.eye(g, dtype=w.dtype)
    return (eye[:, None, :, None] * w[:, :, None, :]).reshape(g * a, g * bb)


def _trunk(x, prompt, caches, wts, tag):
    cache_pool, cache_swa_k, cache_swa_v, cache_diff_k, cache_diff_v = caches
    b, s, d = x.shape
    depth = wts["norm_g"].shape[0]
    norm_g = wts["norm_g"]
    xt = x.reshape(b * s, d)
    pool_s, swa_k, swa_v, diff_k, diff_v, gmlp_v = [], [], [], [], [], []
    pending_mix = None
    past_len = 0 if prompt else cache_diff_k.shape[2]

    for l in range(depth):
        j = l // 2
        f0, f1 = wts["ffn"][l]
        xt = _ffn(xt, norm_g[l, 0], *f0, mix=pending_mix, name=f"{tag}_ffn{l}a")
        pending_mix = None
        if l % 2 == 0:
            if prompt:
                cpool = jnp.zeros((b, POOL_HALO, POOL_W), F32)
                ckv = jnp.zeros((b, SWA_WINDOW, 2 * SWA_KV_W), F32)
            else:
                cpool = jnp.pad(cache_pool[j], ((0, 0), (POOL_HALO - cache_pool.shape[2], 0), (0, 0)))
                ckv = jnp.concatenate([cache_swa_k[j].reshape(b, SWA_WINDOW, SWA_KV_W),
                                       cache_swa_v[j].reshape(b, SWA_WINDOW, SWA_KV_W)], axis=-1)
            x3, pool_o, kv_o = _even_mixer(
                xt.reshape(b, s, d), norm_g[l, 1], wts["even_w_in"][j], wts["pool_wbd"][j],
                wts["pool_scale"][j], wts["swa_sink"][j], wts["even_w_out"][j], cpool, ckv,
                pos0=past_len, has_cache=not prompt, name=f"{tag}_even{l}")
            xt = x3.reshape(b * s, d)
            pool_s.append(pool_o[:, 1:])
            swa_k.append(kv_o[..., :SWA_KV_W].reshape(b, SWA_WINDOW, 2, HEAD_DIM))
            swa_v.append(kv_o[..., SWA_KV_W:].reshape(b, SWA_WINDOW, 2, HEAD_DIM))
            xt = _ffn(xt, norm_g[l, 2], *f1, name=f"{tag}_ffn{l}b")
        else:
            lc = 128 if prompt else s
            outs = _odd_proj(xt, norm_g[l, 1], wts["odd_w_in"][j], wts["gmlp_ln_g"][j], wts["gmlp_ln_b"][j],
                             wts["gmlp_wcat"][(j, lc)], wts["gmlp_bias"][(j, lc)],
                             lc=lc, with_vg=not prompt, name=f"{tag}_oddproj{l}")
            q, k, v, kb, vb, dd = outs[:6]
            diff_k.append(k.reshape(b, s, DIFF_HEADS, DIFF_DH))
            diff_v.append(v.reshape(b, s, DIFF_HEADS, DIFF_DH))
            if not prompt:
                gmlp_v.append(outs[6].reshape(b, s, GMLP_W))
            lam_init = 0.8 - 0.6 * math.exp(-0.3 * l)
            qkv = (q.reshape(b, s, DIFF_W), kb.reshape(b, s, DIFF_W), vb.reshape(b, s, DIFF_W))
            if prompt:
                o = _diff_attn(*qkv, wts["diff_lambda"][j], wts["diff_subln_g"][j],
                               lam_init=lam_init, name=f"{tag}_diffattn{l}")
            else:
                o = _diff_attn_cached(*qkv, cache_diff_k, cache_diff_v, j, wts["diff_lambda"][j],
                                      wts["diff_subln_g"][j], lam_init=lam_init, name=f"{tag}_diffattn{l}")
            w_out = wts["odd_w_out"][j]
            mix = (o.reshape(b * s, DIFF_W), dd, w_out[:DIFF_W], w_out[DIFF_W:])
            last = l == depth - 1
            xt = _ffn(xt, norm_g[l, 2], *f1, mix=mix, final_g=wts["final_g"] if last else None,
                      name=f"{tag}_ffn{l}b")
            pending_mix = None
    if depth % 2 == 1:
        raise NotImplementedError("odd depth")
    return xt.reshape(b, s, d), pool_s, swa_k, swa_v, diff_k, diff_v, gmlp_v


def _prep_weights(norm_g, final_g, ffn_gate, ffn_up, ffn_down, even_w_in, even_w_out, pool_w, pool_scale,
                  swa_sink, odd_w_in, odd_w_out, diff_lambda, diff_subln_g, gmlp_ln_g, gmlp_ln_b,
                  gmlp_w_s, gmlp_b_s, gmlp_chunks):
    depth = norm_g.shape[0]
    ffn = [[(ffn_gate[l, i].astype(BF16), ffn_up[l, i].astype(BF16), ffn_down[l, i].astype(BF16))
            for i in range(2)] for l in range(depth)]
    n_even, n_odd = even_w_in.shape[0], odd_w_in.shape[0]
    wcat, bias = {}, {}
    for j in range(n_odd):
        for lc in gmlp_chunks:
            ws = gmlp_w_s[j][:, :lc, :lc]
            wcat[(j, lc)] = jnp.transpose(ws, (1, 0, 2)).reshape(lc, GMLP_GROUPS * lc)
            bias[(j, lc)] = jnp.repeat(gmlp_b_s[j][:, :lc].T, GMLP_GW, axis=1)
    return dict(
        norm_g=norm_g, final_g=final_g, ffn=ffn,
        even_w_in=[even_w_in[j].astype(BF16) for j in range(n_even)],
        even_w_out=[even_w_out[j].astype(BF16) for j in range(n_even)],
        pool_wbd=[_block_diag(pool_w[j]).astype(BF16) for j in range(n_even)],
        pool_scale=pool_scale, swa_sink=swa_sink,
        odd_w_in=[odd_w_in[j].astype(BF16) for j in range(n_odd)],
        odd_w_out=[odd_w_out[j].astype(BF16) for j in range(n_odd)],
        diff_lambda=diff_lambda, diff_subln_g=diff_subln_g, gmlp_ln_g=gmlp_ln_g, gmlp_ln_b=gmlp_ln_b,
        gmlp_wcat=wcat, gmlp_bias=bias,
    )


def kernel(x_prompt, x_sample, cache_pool, cache_swa_k, cache_swa_v, cache_diff_k, cache_diff_v, norm_g, final_g, ffn_gate, ffn_up, ffn_down, even_w_in, even_w_out, pool_w, pool_scale, swa_sink, odd_w_in, odd_w_out, diff_lambda, diff_subln_g, gmlp_ln_g, gmlp_ln_b, gmlp_w_s, gmlp_b_s):
    wts = _prep_weights(norm_g, final_g, ffn_gate, ffn_up, ffn_down, even_w_in, even_w_out, pool_w,
                        pool_scale, swa_sink, odd_w_in, odd_w_out, diff_lambda, diff_subln_g, gmlp_ln_g,
                        gmlp_ln_b, gmlp_w_s, gmlp_b_s, gmlp_chunks=(128, x_sample.shape[1]))
    caches = (cache_pool, cache_swa_k, cache_swa_v, cache_diff_k, cache_diff_v)
    y_p, pool_p, swa_k_p, swa_v_p, diff_k_p, diff_v_p, _ = _trunk(x_prompt, True, caches, wts, "p")
    y_s, pool_s, swa_k_s, swa_v_s, diff_k_s, diff_v_s, gmlp_v_s = _trunk(x_sample, False, caches, wts, "s")
    return (y_p, y_s,
            jnp.stack(pool_p), jnp.stack(pool_s),
            jnp.stack(swa_k_p), jnp.stack(swa_k_s),
            jnp.stack(swa_v_p), jnp.stack(swa_v_s),
            jnp.stack(diff_k_p), jnp.stack(diff_k_s),
            jnp.stack(diff_v_p), jnp.stack(diff_v_s),
            jnp.stack(gmlp_v_s))
```

```python
import functools
import math

import jax
import jax.numpy as jnp
from jax import lax
from jax.experimental import pallas as pl
from jax.experimental.pallas import tpu as pltpu

F32 = jnp.float32
BF16 = jnp.bfloat16

EPS = 1e-6
CHUNK = 64
HEAD_DIM = 64
POOL_WINDOWS = (2, 4, 8, 16)
POOL_GW = 96
POOL_W = 4 * POOL_GW
POOL_HALO = 16
SWA_WINDOW = 128
SWA_KEYS = 256
SWA_HEADS = 16
SWA_Q_W = SWA_HEADS * HEAD_DIM
SWA_KV_W = 2 * HEAD_DIM
DIFF_HEADS = 8
DIFF_DH = 2 * HEAD_DIM
DIFF_W = DIFF_HEADS * DIFF_DH
GMLP_GROUPS = 4
GMLP_GW = 96
GMLP_W = GMLP_GROUPS * GMLP_GW
SM_SCALE = HEAD_DIM ** -0.5
LOG2_E = math.log2(math.e)

V7X_VMEM_LIMIT_BYTES = 56 * 1024 * 1024
NEG_INF = float("-inf")


def _rms(x, g):
    return x * lax.rsqrt(jnp.mean(x * x, axis=-1, keepdims=True) + EPS) * g


def _dot(a, b):
    return jnp.dot(a, b, preferred_element_type=F32)


def _dot_nt(a, b):
    return lax.dot_general(a, b, (((1,), (1,)), ((), ())), preferred_element_type=F32)


def _resident(shape):
    nd = len(shape)
    return pl.BlockSpec(shape, lambda *_: (0,) * nd, pipeline_mode=pl.Buffered(1))


def _row_tile(n_rows, want):
    t = min(want, n_rows)
    assert n_rows % t == 0, (n_rows, t)
    return t


def _ffn_kernel(*refs, ff_chunk, steps, has_mix, has_final):
    n = len(steps)
    it = iter(refs)
    x_refs = [next(it) for _ in range(n)]
    if has_mix:
        o_refs = [next(it) for _ in range(n)]
        d_refs = [next(it) for _ in range(n)]
        wo_ref = next(it)
    g_ref, wg_ref, wu_ref, wd_ref = next(it), next(it), next(it), next(it)
    if has_final:
        fg_ref = next(it)
    out_refs = [next(it) for _ in range(n)]
    acc_ref = next(it)

    def run(s):
        x = x_refs[s][...]
        if has_mix:
            n_o = o_refs[s].shape[1]
            x = x + _dot(o_refs[s][...], wo_ref[0:n_o, :]) + _dot(d_refs[s][...], wo_ref[n_o:, :])
        hb = _rms(x, g_ref[...]).astype(BF16)
        d_ff = wg_ref.shape[1]
        for c in range(d_ff // ff_chunk):
            sl = slice(c * ff_chunk, (c + 1) * ff_chunk)
            gate = _dot(hb, wg_ref[:, sl])
            up = _dot(hb, wu_ref[:, sl])
            act = (gate * jax.nn.sigmoid(gate) * up).astype(BF16)
            part = _dot(act, wd_ref[sl, :])
            if c == 0:
                acc_ref[...] = part
            else:
                acc_ref[...] += part
        y = x + 0.5 * acc_ref[...]
        if has_final:
            y = _rms(y, fg_ref[...])
        out_refs[s][...] = y

    i = pl.program_id(0)
    first = 0
    for s in range(n):
        pl.when((i >= first) & (i < first + steps[s]))(functools.partial(run, s))
        first += steps[s]


def _stacked(w, idx):
    k = len(idx)
    return pl.BlockSpec((None,) * k + tuple(w.shape[k:]), lambda *_: tuple(idx) + (0,) * (w.ndim - k),
                        pipeline_mode=pl.Buffered(1))


def _ffn(xs, g, ffn_w, layer, mixes=None, w_out=None, final_g=None, *, name):
    d = xs[0].shape[1]
    wg, wu, wd = ffn_w
    d_ff = wg.shape[-1]
    ff_chunk = 256
    assert d_ff % ff_chunk == 0
    tm = _row_tile(min(x.shape[0] for x in xs), 512)
    steps = [x.shape[0] // tm for x in xs]
    starts = [sum(steps[:s]) for s in range(len(xs))]

    def rows(s, w):
        return pl.BlockSpec((tm, w), lambda i: (jnp.clip(i - starts[s], 0, steps[s] - 1), 0))

    args, specs = list(xs), [rows(s, d) for s in range(len(xs))]
    if mixes is not None:
        for k in range(2):
            args += [m[k] for m in mixes]
            specs += [rows(s, m[k].shape[1]) for s, m in enumerate(mixes)]
        args.append(w_out)
        specs.append(_resident(w_out.shape))
    args += [g.reshape(1, d), wg, wu, wd]
    specs += [_resident((1, d)), _stacked(wg, layer), _stacked(wu, layer), _stacked(wd, layer)]
    if final_g is not None:
        args.append(final_g.reshape(1, d))
        specs.append(_resident((1, d)))
    kern = functools.partial(_ffn_kernel, ff_chunk=ff_chunk, steps=tuple(steps), has_mix=mixes is not None,
                             has_final=final_g is not None)
    return list(pl.pallas_call(
        kern,
        grid=(sum(steps),),
        in_specs=specs,
        out_specs=[rows(s, d) for s in range(len(xs))],
        out_shape=[jax.ShapeDtypeStruct(x.shape, F32) for x in xs],
        scratch_shapes=[pltpu.VMEM((tm, d), F32)],
        compiler_params=pltpu.CompilerParams(dimension_semantics=("arbitrary",),
                                             vmem_limit_bytes=V7X_VMEM_LIMIT_BYTES),
        name=name,
    )(*args))


def _even_kernel(sink_ref, x_ref, g_ref, win_ref, pw_ref, ps_ref, wout_ref, cpool_ref, ckv_ref,
                 xo_ref, pool_o_ref, kv_o_ref,
                 e_ref, kv_ref, qb_ref, bo_ref, *, ts, tq, pos0, has_cache):
    sb = pl.program_id(1)
    n_sub = ts // tq
    kv_rows = kv_ref.shape[0]

    @pl.when(sb == 0)
    def _():
        e_ref[0:POOL_HALO, :] = cpool_ref[0]
        kv_ref[0:SWA_WINDOW, :] = ckv_ref[0]
        if kv_rows > SWA_WINDOW + ts:
            kv_ref[SWA_WINDOW + ts:, :] = jnp.zeros((kv_rows - SWA_WINDOW - ts, 2 * SWA_KV_W), F32)

    x = x_ref[0]
    hb = _rms(x, g_ref[...]).astype(BF16)
    u = _dot(hb, win_ref[:, 0:POOL_W])
    qb_ref[...] = (_dot(hb, win_ref[:, POOL_W:POOL_W + SWA_Q_W]) * (SM_SCALE * LOG2_E)).astype(BF16)
    kv_ref[SWA_WINDOW:SWA_WINDOW + ts, :] = _dot(hb, win_ref[:, POOL_W + SWA_Q_W:])
    e_ref[POOL_HALO:POOL_HALO + ts, :] = u

    ev = e_ref[...]
    s2 = ev + pltpu.roll(ev, 1, 0)
    s4 = s2 + pltpu.roll(s2, 2, 0)
    s8 = s4 + pltpu.roll(s4, 4, 0)
    s16 = s8 + pltpu.roll(s8, 8, 0)
    lane = lax.broadcasted_iota(jnp.int32, (1, POOL_W), 1)
    g0, g1, g2 = lane < POOL_GW, lane < 2 * POOL_GW, lane < 3 * POOL_GW
    sw = jnp.where(g0, s2, jnp.where(g1, s4, jnp.where(g2, s8, s16)))[POOL_HALO:]
    win = jnp.where(g0, POOL_WINDOWS[0], jnp.where(g1, POOL_WINDOWS[1],
                                                  jnp.where(g2, POOL_WINDOWS[2], POOL_WINDOWS[3])))
    pos = pos0 + sb * ts + lax.broadcasted_iota(jnp.int32, (ts, 1), 0)
    cnt = jnp.minimum(pos + 1, win).astype(F32)
    diff = sw / cnt - u
    a_out = _dot(diff.astype(BF16), pw_ref[...]) * ps_ref[...]
    new_halo = ev[ts:ts + POOL_HALO]
    e_ref[0:POOL_HALO, :] = new_halo
    pool_o_ref[0] = new_halo

    lane128 = lax.broadcasted_iota(jnp.int32, (1, 2 * HEAD_DIM), 1)
    lo = lane128 < HEAD_DIM
    rq = lax.broadcasted_iota(jnp.int32, (tq, 2 * SWA_KEYS), 0) // CHUNK
    col = lax.broadcasted_iota(jnp.int32, (tq, 2 * SWA_KEYS), 1)
    ek = jnp.where(col >= SWA_KEYS, col - SWA_KEYS, col)
    ekc = ek // CHUNK
    band = (rq <= ekc) & (ekc <= rq + SWA_WINDOW // CHUNK)

    def halves(xa, xb):
        return jnp.concatenate([jnp.where(lo, xa, 0.0), jnp.where(lo, 0.0, xb)], axis=0).astype(BF16)

    row_lo = lax.broadcasted_iota(jnp.int32, (2 * HEAD_DIM, 1), 0) < HEAD_DIM

    def halves_t(xa, xb):
        return jnp.concatenate([jnp.where(row_lo, xa, 0.0), jnp.where(row_lo, 0.0, xb)], axis=1).astype(BF16)

    def sub_block(a, carry):
        r0 = pl.multiple_of(a * tq, tq)
        kve = kv_ref[pl.ds(r0, SWA_KEYS), :]
        ke, ve = kve[:, :SWA_KV_W], kve[:, SWA_KV_W:]
        ve_r = pltpu.roll(ve, HEAD_DIM, 1)
        v2 = (halves(ve, ve_r), halves(ve_r, ve))
        ket = ke.T
        ket_r = pltpu.roll(ket, HEAD_DIM, 0)
        k2t = (halves_t(ket, ket_r), halves_t(ket_r, ket))
        if has_cache:
            valid = band
        else:
            valid = band & ((ek >= SWA_WINDOW) | (sb * n_sub + a > 0))
        n_pairs = SWA_HEADS // 2

        def logits(j):
            s = _dot(qb_ref[pl.ds(r0, tq), 128 * j:128 * (j + 1)], k2t[(2 * j) // n_pairs])
            s = jnp.where(valid, s, NEG_INF)
            sinks = [sink_ref[2 * j + half] * LOG2_E for half in range(2)]
            ms = [jnp.maximum(jnp.max(s[:, half * SWA_KEYS:(half + 1) * SWA_KEYS], axis=-1, keepdims=True),
                              sinks[half]) for half in range(2)]
            return s, sinks, ms

        cur = logits(0)
        for j in range(n_pairs):
            nxt = logits(j + 1) if j + 1 < n_pairs else None
            s, sinks, ms = cur
            ps, dens = [], []
            for half in range(2):
                p = jnp.exp2(s[:, half * SWA_KEYS:(half + 1) * SWA_KEYS] - ms[half])
                ps.append(p)
                dens.append(jnp.sum(p, axis=-1, keepdims=True) + jnp.exp2(sinks[half] - ms[half]))
            p = jnp.concatenate(ps, axis=1).astype(BF16)
            o = _dot(p, v2[(2 * j) // n_pairs])
            o = o / jnp.where(lo, dens[0], dens[1])
            bo_ref[pl.ds(r0, tq), 128 * j:128 * (j + 1)] = o.astype(BF16)
            cur = nxt
        return carry

    lax.fori_loop(0, n_sub, sub_block, 0)

    y = _dot(a_out.astype(BF16), wout_ref[0:POOL_W, :]) + _dot(bo_ref[...], wout_ref[POOL_W:, :])
    xo_ref[0] = x + y

    new_win = kv_ref[ts:ts + SWA_WINDOW, :]
    kv_ref[0:SWA_WINDOW, :] = new_win
    kv_o_ref[0] = new_win


def _even_mixer(x, g, w_in, pool_wbd, pool_scale, sink, w_out, cache_pool, cache_kv, *, pos0, has_cache, name):
    b, s, d = x.shape
    ts = _row_tile(s, 512)
    tq = min(ts, SWA_WINDOW)
    kv_rows = max(SWA_WINDOW + ts, (ts // tq - 1) * tq + SWA_KEYS)
    kern = functools.partial(_even_kernel, ts=ts, tq=tq, pos0=pos0, has_cache=has_cache)
    per_b = lambda r, w: pl.BlockSpec((1, r, w), lambda bi, si: (bi, 0, 0))
    return pl.pallas_call(
        kern,
        grid=(b, s // ts),
        in_specs=[
            pl.BlockSpec(memory_space=pltpu.SMEM),
            pl.BlockSpec((1, ts, d), lambda bi, si: (bi, si, 0)),
            _resident((1, d)), _resident(w_in.shape), _resident(pool_wbd.shape), _resident((1, POOL_W)),
            _resident(w_out.shape),
            per_b(POOL_HALO, POOL_W), per_b(SWA_WINDOW, 2 * SWA_KV_W),
        ],
        out_specs=[
            pl.BlockSpec((1, ts, d), lambda bi, si: (bi, si, 0)),
            per_b(POOL_HALO, POOL_W), per_b(SWA_WINDOW, 2 * SWA_KV_W),
        ],
        out_shape=[
            jax.ShapeDtypeStruct((b, s, d), F32),
            jax.ShapeDtypeStruct((b, POOL_HALO, POOL_W), F32),
            jax.ShapeDtypeStruct((b, SWA_WINDOW, 2 * SWA_KV_W), F32),
        ],
        scratch_shapes=[
            pltpu.VMEM((POOL_HALO + ts, POOL_W), F32),
            pltpu.VMEM((kv_rows, 2 * SWA_KV_W), F32),
            pltpu.VMEM((ts, SWA_Q_W), BF16),
            pltpu.VMEM((ts, SWA_Q_W), BF16),
        ],
        compiler_params=pltpu.CompilerParams(dimension_semantics=("arbitrary", "arbitrary"),
                                             vmem_limit_bytes=V7X_VMEM_LIMIT_BYTES),
        name=name,
    )(sink, x, g.reshape(1, d), w_in, pool_wbd, pool_scale.reshape(1, POOL_W), w_out, cache_pool, cache_kv)


def _gelu(x):
    return 0.5 * x * (1.0 + lax.erf(x * math.sqrt(0.5)))


def _odd_proj_kernel(*refs, lc, with_vg):
    (x_ref, g_ref, w_ref, lng_ref, lnb_ref, wcat_ref, bm_ref,
     q_ref, k_ref, v_ref, kb_ref, vb_ref, d_ref) = refs[:13]
    vg_ref = refs[13] if with_vg else None
    tm = x_ref.shape[0]

    hb = _rms(x_ref[...], g_ref[...]).astype(BF16)
    q_ref[...] = (_dot(hb, w_ref[:, 0:DIFF_W]) * (SM_SCALE * LOG2_E)).astype(BF16)
    k = _dot(hb, w_ref[:, DIFF_W:2 * DIFF_W])
    k_ref[...] = k
    kb_ref[...] = k.astype(BF16)
    v = _dot(hb, w_ref[:, 2 * DIFF_W:3 * DIFF_W])
    v_ref[...] = v
    vb_ref[...] = v.astype(BF16)

    u = _gelu(_dot(hb, w_ref[:, 3 * DIFF_W:3 * DIFF_W + GMLP_W]))
    gv = _gelu(_dot(hb, w_ref[:, 3 * DIFF_W + GMLP_W:]))
    mu = jnp.mean(gv, axis=-1, keepdims=True)
    cen = gv - mu
    var = jnp.mean(cen * cen, axis=-1, keepdims=True)
    vg = cen * lax.rsqrt(var + EPS) * lng_ref[...] + lnb_ref[...]
    if with_vg:
        vg_ref[...] = vg

    t_i = lax.broadcasted_iota(jnp.int32, (lc, GMLP_GROUPS * lc), 0)
    s_i = lax.broadcasted_iota(jnp.int32, (lc, GMLP_GROUPS * lc), 1) % lc
    wcat = jnp.where(s_i <= t_i, wcat_ref[...], 0.0).astype(BF16)
    lane = lax.broadcasted_iota(jnp.int32, (1, GMLP_W), 1) // GMLP_GW
    for c in range(tm // lc):
        rows = slice(c * lc, (c + 1) * lc)
        vc = vg[rows]
        vstack = jnp.concatenate([jnp.where(lane == gi, vc, 0.0) for gi in range(GMLP_GROUPS)],
                                 axis=0).astype(BF16)
        mix = _dot(wcat, vstack) + bm_ref[...]
        d_ref[rows, :] = (u[rows] * mix).astype(BF16)


def _odd_proj(x, g, w_in, ln_g, ln_b, wcat, bias_m, *, lc, with_vg, name):
    t, d = x.shape
    tm = _row_tile(t, 512)
    assert tm % lc == 0
    row = lambda w: pl.BlockSpec((tm, w), lambda i: (i, 0))
    out_shape = [
        jax.ShapeDtypeStruct((t, DIFF_W), BF16),
        jax.ShapeDtypeStruct((t, DIFF_W), F32),
        jax.ShapeDtypeStruct((t, DIFF_W), F32),
        jax.ShapeDtypeStruct((t, DIFF_W), BF16),
        jax.ShapeDtypeStruct((t, DIFF_W), BF16),
        jax.ShapeDtypeStruct((t, GMLP_W), BF16),
    ]
    out_specs = [row(DIFF_W)] * 5 + [row(GMLP_W)]
    if with_vg:
        out_shape.append(jax.ShapeDtypeStruct((t, GMLP_W), F32))
        out_specs.append(row(GMLP_W))
    return pl.pallas_call(
        functools.partial(_odd_proj_kernel, lc=lc, with_vg=with_vg),
        grid=(t // tm,),
        in_specs=[row(d), _resident((1, d)), _resident(w_in.shape), _resident((1, GMLP_W)),
                  _resident((1, GMLP_W)), _resident(wcat.shape), _resident(bias_m.shape)],
        out_specs=out_specs,
        out_shape=out_shape,
        compiler_params=pltpu.CompilerParams(dimension_semantics=("arbitrary",),
                                             vmem_limit_bytes=V7X_VMEM_LIMIT_BYTES),
        name=name,
    )(x, g.reshape(1, d), w_in, ln_g.reshape(1, GMLP_W), ln_b.reshape(1, GMLP_W), wcat, bias_m)


def _head_lanes(g):
    return slice(g * DIFF_DH, (g + 1) * DIFF_DH)


def _diff_lambda(lam_ref, lam_init):
    lp = lam_ref[...]
    return (jnp.exp(jnp.sum(lp[0:1] * lp[1:2], axis=-1, keepdims=True))
            - jnp.exp(jnp.sum(lp[2:3] * lp[3:4], axis=-1, keepdims=True)) + lam_init)


def _diff_diag_mask(tq):
    return (lax.broadcasted_iota(jnp.int32, (tq, 2 * tq), 0) // CHUNK
            <= (lax.broadcasted_iota(jnp.int32, (tq, 2 * tq), 1) % tq) // CHUNK)


def _diff_start(state, g, qi, tq):
    qs_ref, m_ref, l_ref, acc_ref = state
    d_lo = lax.broadcasted_iota(jnp.int32, (DIFF_DH, 2 * tq), 0) < HEAD_DIM
    c_lo = lax.broadcasted_iota(jnp.int32, (DIFF_DH, 2 * tq), 1) < tq
    qt = jnp.concatenate([qi, qi], axis=0).astype(F32).T
    qs_ref[g] = jnp.where(d_lo == c_lo, qt, 0.0).astype(BF16)
    if m_ref is not None:
        m_ref[g] = jnp.full((1, 2 * tq), NEG_INF, F32)
    l_ref[g] = jnp.zeros((1, 2 * tq), F32)
    acc_ref[g] = jnp.zeros((DIFF_DH, 2 * tq), F32)


def _diff_scores(state, g, kb):
    return _dot(kb, state[0][g])


def _diff_update(state, g, s, vt, mask):
    _, m_ref, l_ref, acc_ref = state
    if mask is not None:
        s = jnp.where(mask, s, NEG_INF)
    m_prev = m_ref[g]
    m_new = jnp.maximum(m_prev, jnp.max(s, axis=0, keepdims=True))
    alpha = jnp.exp2(m_prev - m_new)
    p = jnp.exp2(s - m_new)
    l_ref[g] = alpha * l_ref[g] + jnp.sum(p, axis=0, keepdims=True)
    acc_ref[g] = alpha * acc_ref[g] + _dot(vt, p.astype(BF16))
    m_ref[g] = m_new


def _diff_heads_step(state, n_heads, s_first, key_block, value_block, mask, s_after):
    s_cur, s_nxt = s_first, None
    for g in range(n_heads):
        if g + 1 < n_heads:
            s_nxt = _diff_scores(state, g + 1, key_block(g + 1))
        elif s_after is not None:
            s_nxt = s_after()
        _diff_update(state, g, s_cur, value_block(g), mask)
        s_cur = s_nxt
    return s_nxt


def _diff_finish(state, g, lam, sg, tq, lam_init):
    _, _, l_ref, acc_ref = state
    ot = acc_ref[g] / l_ref[g]
    dt = ot - lam * pltpu.roll(ot, tq, 1)
    dt = dt * lax.rsqrt(jnp.mean(dt * dt, axis=0, keepdims=True) + EPS) * sg
    return ((dt * (1.0 - lam_init)).T[:tq]).astype(BF16)


def _vt(v):
    return v.astype(F32).T.astype(BF16)


def _diff_attn_kernel(lam_ref, sg_ref, q_ref, k_ref, v_ref, o_ref, vt_ref, s_ref, m_ref, a_ref,
                      qs_ref, l_ref, acc_ref, *, tq, n_heads, lam_init):
    nq = q_ref.shape[1] // tq
    heads = range(n_heads)
    state = (qs_ref, None, l_ref, acc_ref)
    lam = _diff_lambda(lam_ref, lam_init)
    diag_ok = _diff_diag_mask(tq)

    for g in heads:
        for c in range(nq):
            vt_ref[g, c] = _vt(v_ref[0, c * tq:(c + 1) * tq, _head_lanes(g)])

    def logits_phase(g, r0, src, dst, mask):
        s = _dot(k_ref[0, pl.ds(r0, tq), _head_lanes(g)], qs_ref[g])
        if mask is not None:
            s = jnp.where(mask, s, NEG_INF)
        s_ref[dst, g] = s
        m_old = m_ref[src, g]
        m_new = jnp.maximum(m_old, jnp.max(s, axis=0, keepdims=True))
        m_ref[dst, g] = m_new
        a_ref[dst, g] = jnp.exp2(m_old - m_new)

    def accum_phase(g, blk, slot):
        p = jnp.exp2(s_ref[slot, g] - m_ref[slot, g])
        alpha = a_ref[slot, g]
        l_ref[g] = alpha * l_ref[g] + jnp.sum(p, axis=0, keepdims=True)
        acc_ref[g] = alpha * acc_ref[g] + _dot(vt_ref[g, blk], p.astype(BF16))

    def step(blk, slot, mask):
        r1 = (blk + 1) * tq if isinstance(blk, int) else pl.multiple_of((blk + 1) * tq, tq)
        logits_phase(0, r1, slot, 1 - slot, mask)
        for g in heads:
            if g + 1 < n_heads:
                logits_phase(g + 1, r1, slot, 1 - slot, mask)
            accum_phase(g, blk, slot)

    def two_steps(t, c):
        step(2 * t, 0, None)
        step(2 * t + 1, 1, None)
        return c

    def q_block(i, odd):
        q0 = i * tq if isinstance(i, int) else pl.multiple_of(i * tq, tq)
        for g in heads:
            _diff_start(state, g, q_ref[0, pl.ds(q0, tq), _head_lanes(g)], tq)
            m_ref[1, g] = jnp.full((1, 2 * tq), NEG_INF, F32)
        if isinstance(i, int) and i == 0:
            for g in heads:
                logits_phase(g, 0, 1, 0, diag_ok)
            last_slot = 0
        else:
            for g in heads:
                logits_phase(g, 0, 1, 0, None)
            lax.fori_loop(0, (i - 1) // 2, two_steps, 0)
            if odd:
                step(i - 1, 0, diag_ok)
                last_slot = 1
            else:
                step(i - 2, 0, None)
                step(i - 1, 1, diag_ok)
                last_slot = 0
        for g in heads:
            accum_phase(g, i, last_slot)
        for g in heads:
            o_ref[0, pl.ds(q0, tq), _head_lanes(g)] = _diff_finish(state, g, lam, sg_ref[...], tq, lam_init)

    q_block(0, False)

    def q_pair(u, c):
        q_block(2 * u + 1, True)
        q_block(2 * u + 2, False)
        return c
    lax.fori_loop(0, (nq - 1) // 2, q_pair, 0)
    if nq % 2 == 0:
        q_block(nq - 1, True)


def _diff_attn_cached_kernel(lam_ref, sg_ref, q_ref, k_ref, v_ref, ck_ref, cv_ref, o_ref, *state, tq, lam_init):
    c = pl.program_id(1)
    heads = range(DIFF_HEADS)

    @pl.when(c == 0)
    def _():
        for g in heads:
            _diff_start(state, g, q_ref[0, :, _head_lanes(g)], tq)

    tkc = ck_ref.shape[0] // DIFF_HEADS
    head_rows = lambda g: pl.ds(g, tkc, stride=DIFF_HEADS)
    cached_keys = lambda g: ck_ref[head_rows(g), :].astype(BF16)
    _diff_heads_step(state, DIFF_HEADS, _diff_scores(state, 0, cached_keys(0)), cached_keys,
                     lambda g: cv_ref[head_rows(g), :].T.astype(BF16), None, None)

    @pl.when(c == pl.num_programs(1) - 1)
    def _():
        lam = _diff_lambda(lam_ref, lam_init)
        new_keys = lambda g: k_ref[0, :, _head_lanes(g)]
        _diff_heads_step(state, DIFF_HEADS, _diff_scores(state, 0, new_keys(0)), new_keys,
                         lambda g: _vt(v_ref[0, :, _head_lanes(g)]), _diff_diag_mask(tq), None)
        for g in heads:
            o_ref[0, :, _head_lanes(g)] = _diff_finish(state, g, lam, sg_ref[...], tq, lam_init)


def _diff_state_scratch(n_heads, tq):
    return [pltpu.VMEM((n_heads, DIFF_DH, 2 * tq), BF16),
            pltpu.VMEM((n_heads, 1, 2 * tq), F32),
            pltpu.VMEM((n_heads, 1, 2 * tq), F32),
            pltpu.VMEM((n_heads, DIFF_DH, 2 * tq), F32)]


def _diff_attn(q, kb, vb, lam_p, subln_g, *, lam_init, name):
    b, s, _ = q.shape
    tq = _row_tile(s, 256)
    n_heads = 4
    grp = pl.BlockSpec((1, s, n_heads * DIFF_DH), lambda bi, hi: (bi, 0, hi))
    return pl.pallas_call(
        functools.partial(_diff_attn_kernel, tq=tq, n_heads=n_heads, lam_init=lam_init),
        grid=(b, DIFF_HEADS // n_heads),
        in_specs=[_resident(lam_p.shape), _resident((DIFF_DH, 1)), grp, grp, grp],
        out_specs=grp,
        out_shape=jax.ShapeDtypeStruct((b, s, DIFF_W), BF16),
        scratch_shapes=[pltpu.VMEM((n_heads, s // tq, DIFF_DH, tq), BF16),
                        pltpu.VMEM((2, n_heads, tq, 2 * tq), F32),
                        pltpu.VMEM((2, n_heads, 1, 2 * tq), F32),
                        pltpu.VMEM((2, n_heads, 1, 2 * tq), F32),
                        pltpu.VMEM((n_heads, DIFF_DH, 2 * tq), BF16),
                        pltpu.VMEM((n_heads, 1, 2 * tq), F32),
                        pltpu.VMEM((n_heads, DIFF_DH, 2 * tq), F32)],
        compiler_params=pltpu.CompilerParams(dimension_semantics=("arbitrary", "arbitrary"),
                                             vmem_limit_bytes=V7X_VMEM_LIMIT_BYTES),
        name=name,
    )(lam_p, subln_g.reshape(DIFF_DH, 1), q, kb, vb)


def _diff_attn_cached(q, kb, vb, cache_k, cache_v, layer_j, lam_p, subln_g, *, lam_init, name):
    b, s, _ = q.shape
    assert s % CHUNK == 0 and s <= 256, s
    n_layers, _, lc = cache_k.shape[:3]
    tkc = _row_tile(lc, 1024)
    cache_k, cache_v = (c.reshape(n_layers, b, lc * DIFF_HEADS, DIFF_DH) for c in (cache_k, cache_v))
    new = pl.BlockSpec((1, s, DIFF_W), lambda bi, ci: (bi, 0, 0))
    cache = pl.BlockSpec((None, None, tkc * DIFF_HEADS, DIFF_DH), lambda bi, ci: (layer_j, bi, ci, 0))
    return pl.pallas_call(
        functools.partial(_diff_attn_cached_kernel, tq=s, lam_init=lam_init),
        grid=(b, lc // tkc),
        in_specs=[_resident(lam_p.shape), _resident((DIFF_DH, 1)), new, new, new, cache, cache],
        out_specs=new,
        out_shape=jax.ShapeDtypeStruct((b, s, DIFF_W), BF16),
        scratch_shapes=_diff_state_scratch(DIFF_HEADS, s),
        compiler_params=pltpu.CompilerParams(dimension_semantics=("arbitrary", "arbitrary"),
                                             vmem_limit_bytes=V7X_VMEM_LIMIT_BYTES),
        name=name,
    )(lam_p, subln_g.reshape(DIFF_DH, 1), q, kb, vb, cache_k, cache_v)


def _block_diag(w):
    g, a, bb = w.shape
    eye = jnp.eye(g, dtype=w.dtype)
    return (eye[:, None, :, None] * w[:, :, None, :]).reshape(g * a, g * bb)


def _forward(xs, caches, wts):
    cache_pool, cache_swa_k, cache_swa_v, cache_diff_k, cache_diff_v = caches
    shapes = [x.shape for x in xs]
    d = shapes[0][2]
    norm_g = wts["norm_g"]
    depth = norm_g.shape[0]
    assert depth % 2 == 0
    past_len = cache_diff_k.shape[2]
    tags = ("p", "s")
    xt = [x.reshape(-1, d) for x in xs]
    outs = [dict(pool=[], swa_k=[], swa_v=[], diff_k=[], diff_v=[], gmlp_v=[]) for _ in xs]

    for l in range(depth):
        j = l // 2
        xt = _ffn(xt, norm_g[l, 0], wts["ffn"], (l, 0), name=f"ffn{l}a")
        if l % 2 == 0:
            for si, (b, s, _) in enumerate(shapes):
                prompt = si == 0
                if prompt:
                    cpool = jnp.zeros((b, POOL_HALO, POOL_W), F32)
                    ckv = jnp.zeros((b, SWA_WINDOW, 2 * SWA_KV_W), F32)
                else:
                    cpool = jnp.pad(cache_pool[j], ((0, 0), (POOL_HALO - cache_pool.shape[2], 0), (0, 0)))
                    ckv = jnp.concatenate([cache_swa_k[j].reshape(b, SWA_WINDOW, SWA_KV_W),
                                           cache_swa_v[j].reshape(b, SWA_WINDOW, SWA_KV_W)], axis=-1)
                x3, pool_o, kv_o = _even_mixer(
                    xt[si].reshape(b, s, d), norm_g[l, 1], wts["even_w_in"][j], wts["pool_wbd"][j],
                    wts["pool_scale"][j], wts["swa_sink"][j], wts["even_w_out"][j], cpool, ckv,
                    pos0=0 if prompt else past_len, has_cache=not prompt, name=f"{tags[si]}_even{l}")
                xt[si] = x3.reshape(b * s, d)
                outs[si]["pool"].append(pool_o[:, 1:])
                outs[si]["swa_k"].append(kv_o[..., :SWA_KV_W].reshape(b, SWA_WINDOW, 2, HEAD_DIM))
                outs[si]["swa_v"].append(kv_o[..., SWA_KV_W:].reshape(b, SWA_WINDOW, 2, HEAD_DIM))
            xt = _ffn(xt, norm_g[l, 2], wts["ffn"], (l, 1), name=f"ffn{l}b")
        else:
            lam_init = 0.8 - 0.6 * math.exp(-0.3 * l)
            mixes = []
            for si, (b, s, _) in enumerate(shapes):
                prompt = si == 0
                lc = 128 if prompt else s
                res = _odd_proj(xt[si], norm_g[l, 1], wts["odd_w_in"][j], wts["gmlp_ln_g"][j],
                                wts["gmlp_ln_b"][j], wts["gmlp_wcat"][(j, lc)], wts["gmlp_bias"][(j, lc)],
                                lc=lc, with_vg=not prompt, name=f"{tags[si]}_oddproj{l}")
                q, k, v, kb, vb, dd = res[:6]
                outs[si]["diff_k"].append(k.reshape(b, s, DIFF_HEADS, DIFF_DH))
                outs[si]["diff_v"].append(v.reshape(b, s, DIFF_HEADS, DIFF_DH))
                if not prompt:
                    outs[si]["gmlp_v"].append(res[6].reshape(b, s, GMLP_W))
                qkv = (q.reshape(b, s, DIFF_W), kb.reshape(b, s, DIFF_W), vb.reshape(b, s, DIFF_W))
                if prompt:
                    o = _diff_attn(*qkv, wts["diff_lambda"][j], wts["diff_subln_g"][j],
                                   lam_init=lam_init, name=f"{tags[si]}_diffattn{l}")
                else:
                    o = _diff_attn_cached(*qkv, cache_diff_k, cache_diff_v, j, wts["diff_lambda"][j],
                                          wts["diff_subln_g"][j], lam_init=lam_init,
                                          name=f"{tags[si]}_diffattn{l}")
                mixes.append((o.reshape(b * s, DIFF_W), dd))
            xt = _ffn(xt, norm_g[l, 2], wts["ffn"], (l, 1), mixes=mixes, w_out=wts["odd_w_out"][j],
                      final_g=wts["final_g"] if l == depth - 1 else None, name=f"ffn{l}b")
    return [x.reshape(shape) for x, shape in zip(xt, shapes)], outs


def _prep_weights(norm_g, final_g, ffn_gate, ffn_up, ffn_down, even_w_in, even_w_out, pool_w, pool_scale,
                  swa_sink, odd_w_in, odd_w_out, diff_lambda, diff_subln_g, gmlp_ln_g, gmlp_ln_b,
                  gmlp_w_s, gmlp_b_s, gmlp_chunks):
    ffn = (ffn_gate.astype(BF16), ffn_up.astype(BF16), ffn_down.astype(BF16))
    n_even, n_odd = even_w_in.shape[0], odd_w_in.shape[0]
    wcat, bias = {}, {}
    for j in range(n_odd):
        for lc in gmlp_chunks:
            ws = gmlp_w_s[j][:, :lc, :lc]
            wcat[(j, lc)] = jnp.transpose(ws, (1, 0, 2)).reshape(lc, GMLP_GROUPS * lc)
            bias[(j, lc)] = jnp.repeat(gmlp_b_s[j][:, :lc].T, GMLP_GW, axis=1)
    return dict(
        norm_g=norm_g, final_g=final_g, ffn=ffn,
        even_w_in=[even_w_in[j].astype(BF16) for j in range(n_even)],
        even_w_out=[even_w_out[j].astype(BF16) for j in range(n_even)],
        pool_wbd=[_block_diag(pool_w[j]).astype(BF16) for j in range(n_even)],
        pool_scale=pool_scale, swa_sink=swa_sink,
        odd_w_in=[odd_w_in[j].astype(BF16) for j in range(n_odd)],
        odd_w_out=[odd_w_out[j].astype(BF16) for j in range(n_odd)],
        diff_lambda=diff_lambda, diff_subln_g=diff_subln_g, gmlp_ln_g=gmlp_ln_g, gmlp_ln_b=gmlp_ln_b,
        gmlp_wcat=wcat, gmlp_bias=bias,
    )


def kernel(x_prompt, x_sample, cache_pool, cache_swa_k, cache_swa_v, cache_diff_k, cache_diff_v, norm_g, final_g, ffn_gate, ffn_up, ffn_down, even_w_in, even_w_out, pool_w, pool_scale, swa_sink, odd_w_in, odd_w_out, diff_lambda, diff_subln_g, gmlp_ln_g, gmlp_ln_b, gmlp_w_s, gmlp_b_s):
    wts = _prep_weights(norm_g, final_g, ffn_gate, ffn_up, ffn_down, even_w_in, even_w_out, pool_w,
                        pool_scale, swa_sink, odd_w_in, odd_w_out, diff_lambda, diff_subln_g, gmlp_ln_g,
                        gmlp_ln_b, gmlp_w_s, gmlp_b_s, gmlp_chunks=(128, x_sample.shape[1]))
    caches = (cache_pool, cache_swa_k, cache_swa_v, cache_diff_k, cache_diff_v)
    (y_p, y_s), (out_p, out_s) = _forward((x_prompt, x_sample), caches, wts)
    return (y_p, y_s,
            jnp.stack(out_p["pool"]), jnp.stack(out_s["pool"]),
            jnp.stack(out_p["swa_k"]), jnp.stack(out_s["swa_k"]),
            jnp.stack(out_p["swa_v"]), jnp.stack(out_s["swa_v"]),
            jnp.stack(out_p["diff_k"]), jnp.stack(out_s["diff_k"]),
            jnp.stack(out_p["diff_v"]), jnp.stack(out_s["diff_v"]),
            jnp.stack(out_s["gmlp_v"]))
```

```python
import functools
import math

import jax
import jax.numpy as jnp
from jax import lax
from jax.experimental import pallas as pl
from jax.experimental.pallas import tpu as pltpu

F32 = jnp.float32
BF16 = jnp.bfloat16

EPS = 1e-6
CHUNK = 64
HEAD_DIM = 64
POOL_WINDOWS = (2, 4, 8, 16)
POOL_GW = 96
POOL_W = 4 * POOL_GW
POOL_HALO = 16
SWA_WINDOW = 128
SWA_KEYS = 256
SWA_HEADS = 16
SWA_LOOKAHEAD = 3
SWA_Q_W = SWA_HEADS * HEAD_DIM
SWA_KV_W = 2 * HEAD_DIM
DIFF_HEADS = 8
DIFF_DH = 2 * HEAD_DIM
DIFF_W = DIFF_HEADS * DIFF_DH
DIFF_ACC_ROWS = DIFF_DH + 16
GMLP_GROUPS = 4
GMLP_GW = 96
GMLP_W = GMLP_GROUPS * GMLP_GW
SM_SCALE = HEAD_DIM ** -0.5
LOG2_E = math.log2(math.e)

V7X_VMEM_LIMIT_BYTES = 56 * 1024 * 1024
NEG_INF = float("-inf")


def _rms(x, g):
    return x * lax.rsqrt(jnp.mean(x * x, axis=-1, keepdims=True) + EPS) * g


def _dot(a, b):
    return jnp.dot(a, b, preferred_element_type=F32)


def _dot_nt(a, b):
    return lax.dot_general(a, b, (((1,), (1,)), ((), ())), preferred_element_type=F32)


def _resident(shape):
    nd = len(shape)
    return pl.BlockSpec(shape, lambda *_: (0,) * nd, pipeline_mode=pl.Buffered(1))


def _row_tile(n_rows, want):
    t = min(want, n_rows)
    assert n_rows % t == 0, (n_rows, t)
    return t


def _ffn_kernel(*refs, ff_chunk, steps, has_mix, has_final):
    n = len(steps)
    it = iter(refs)
    x_refs = [next(it) for _ in range(n)]
    if has_mix:
        o_refs = [next(it) for _ in range(n)]
        d_refs = [next(it) for _ in range(n)]
        wo_ref = next(it)
    g_ref, wg_ref, wu_ref, wd_ref = next(it), next(it), next(it), next(it)
    if has_final:
        fg_ref = next(it)
    out_refs = [next(it) for _ in range(n)]
    acc_ref = next(it)

    def run(s):
        x = x_refs[s][...]
        if has_mix:
            n_o = o_refs[s].shape[1]
            x = x + _dot(o_refs[s][...], wo_ref[0:n_o, :]) + _dot(d_refs[s][...], wo_ref[n_o:, :])
        hb = _rms(x, g_ref[...]).astype(BF16)
        d_ff = wg_ref.shape[1]
        for c in range(d_ff // ff_chunk):
            sl = slice(c * ff_chunk, (c + 1) * ff_chunk)
            gate = _dot(hb, wg_ref[:, sl])
            up = _dot(hb, wu_ref[:, sl])
            act = (gate * jax.nn.sigmoid(gate) * up).astype(BF16)
            part = _dot(act, wd_ref[sl, :])
            if c == 0:
                acc_ref[...] = part
            else:
                acc_ref[...] += part
        y = x + 0.5 * acc_ref[...]
        if has_final:
            y = _rms(y, fg_ref[...])
        out_refs[s][...] = y

    i = pl.program_id(0)
    first = 0
    for s in range(n):
        pl.when((i >= first) & (i < first + steps[s]))(functools.partial(run, s))
        first += steps[s]


def _stacked(w, idx):
    k = len(idx)
    return pl.BlockSpec((None,) * k + tuple(w.shape[k:]), lambda *_: tuple(idx) + (0,) * (w.ndim - k),
                        pipeline_mode=pl.Buffered(1))


def _ffn(xs, g, ffn_w, layer, mixes=None, w_out=None, final_g=None, *, name):
    d = xs[0].shape[1]
    wg, wu, wd = ffn_w
    d_ff = wg.shape[-1]
    ff_chunk = 256
    assert d_ff % ff_chunk == 0
    tm = _row_tile(min(x.shape[0] for x in xs), 512)
    steps = [x.shape[0] // tm for x in xs]
    starts = [sum(steps[:s]) for s in range(len(xs))]

    def rows(s, w):
        return pl.BlockSpec((tm, w), lambda i: (jnp.clip(i - starts[s], 0, steps[s] - 1), 0))

    args, specs = list(xs), [rows(s, d) for s in range(len(xs))]
    if mixes is not None:
        for k in range(2):
            args += [m[k] for m in mixes]
            specs += [rows(s, m[k].shape[1]) for s, m in enumerate(mixes)]
        args.append(w_out)
        specs.append(_resident(w_out.shape))
    args += [g.reshape(1, d), wg, wu, wd]
    specs += [_resident((1, d)), _stacked(wg, layer), _stacked(wu, layer), _stacked(wd, layer)]
    if final_g is not None:
        args.append(final_g.reshape(1, d))
        specs.append(_resident((1, d)))
    kern = functools.partial(_ffn_kernel, ff_chunk=ff_chunk, steps=tuple(steps), has_mix=mixes is not None,
                             has_final=final_g is not None)
    return list(pl.pallas_call(
        kern,
        grid=(sum(steps),),
        in_specs=specs,
        out_specs=[rows(s, d) for s in range(len(xs))],
        out_shape=[jax.ShapeDtypeStruct(x.shape, F32) for x in xs],
        scratch_shapes=[pltpu.VMEM((tm, d), F32)],
        compiler_params=pltpu.CompilerParams(dimension_semantics=("arbitrary",),
                                             vmem_limit_bytes=V7X_VMEM_LIMIT_BYTES),
        name=name,
    )(*args))


def _even_kernel(sink_ref, x_ref, g_ref, win_ref, pw_ref, ps_ref, wout_ref, cpool_ref, ckv_ref,
                 xo_ref, pool_o_ref, kv_o_ref,
                 e_ref, kv_ref, qb_ref, bo_ref, *, ts, tq, pos0, has_cache):
    sb = pl.program_id(1)
    n_sub = ts // tq
    kv_rows = kv_ref.shape[0]

    @pl.when(sb == 0)
    def _():
        e_ref[0:POOL_HALO, :] = cpool_ref[0]
        kv_ref[0:SWA_WINDOW, :] = ckv_ref[0]
        if kv_rows > SWA_WINDOW + ts:
            kv_ref[SWA_WINDOW + ts:, :] = jnp.zeros((kv_rows - SWA_WINDOW - ts, 2 * SWA_KV_W), F32)

    x = x_ref[0]
    hb = _rms(x, g_ref[...]).astype(BF16)
    u = _dot(hb, win_ref[:, 0:POOL_W])
    qb_ref[...] = (_dot(hb, win_ref[:, POOL_W:POOL_W + SWA_Q_W]) * (SM_SCALE * LOG2_E)).astype(BF16)
    kv_ref[SWA_WINDOW:SWA_WINDOW + ts, :] = _dot(hb, win_ref[:, POOL_W + SWA_Q_W:])
    e_ref[POOL_HALO:POOL_HALO + ts, :] = u

    ev = e_ref[...]
    s2 = ev + pltpu.roll(ev, 1, 0)
    s4 = s2 + pltpu.roll(s2, 2, 0)
    s8 = s4 + pltpu.roll(s4, 4, 0)
    s16 = s8 + pltpu.roll(s8, 8, 0)
    lane = lax.broadcasted_iota(jnp.int32, (1, POOL_W), 1)
    g0, g1, g2 = lane < POOL_GW, lane < 2 * POOL_GW, lane < 3 * POOL_GW
    sw = jnp.where(g0, s2, jnp.where(g1, s4, jnp.where(g2, s8, s16)))[POOL_HALO:]
    win = jnp.where(g0, POOL_WINDOWS[0], jnp.where(g1, POOL_WINDOWS[1],
                                                  jnp.where(g2, POOL_WINDOWS[2], POOL_WINDOWS[3])))
    pos = pos0 + sb * ts + lax.broadcasted_iota(jnp.int32, (ts, 1), 0)
    cnt = jnp.minimum(pos + 1, win).astype(F32)
    diff = sw / cnt - u
    a_out = _dot(diff.astype(BF16), pw_ref[...]) * ps_ref[...]
    new_halo = ev[ts:ts + POOL_HALO]
    e_ref[0:POOL_HALO, :] = new_halo
    pool_o_ref[0] = new_halo

    lane128 = lax.broadcasted_iota(jnp.int32, (1, 2 * HEAD_DIM), 1)
    lo = lane128 < HEAD_DIM
    rq = lax.broadcasted_iota(jnp.int32, (tq, 2 * SWA_KEYS), 0) // CHUNK
    col = lax.broadcasted_iota(jnp.int32, (tq, 2 * SWA_KEYS), 1)
    ek = jnp.where(col >= SWA_KEYS, col - SWA_KEYS, col)
    ekc = ek // CHUNK
    band = (rq <= ekc) & (ekc <= rq + SWA_WINDOW // CHUNK)

    def halves(xa, xb):
        return jnp.concatenate([jnp.where(lo, xa, 0.0), jnp.where(lo, 0.0, xb)], axis=0).astype(BF16)

    row_lo = lax.broadcasted_iota(jnp.int32, (2 * HEAD_DIM, 1), 0) < HEAD_DIM

    def halves_t(xa, xb):
        return jnp.concatenate([jnp.where(row_lo, xa, 0.0), jnp.where(row_lo, 0.0, xb)], axis=1).astype(BF16)

    def sub_block(a):
        r0 = a * tq
        kve = kv_ref[pl.ds(r0, SWA_KEYS), :]
        ke, ve = kve[:, :SWA_KV_W], kve[:, SWA_KV_W:]
        ve_r = pltpu.roll(ve, HEAD_DIM, 1)
        v2 = (halves(ve, ve_r), halves(ve_r, ve))
        ket = ke.T
        ket_r = pltpu.roll(ket, HEAD_DIM, 0)
        k2t = (halves_t(ket, ket_r), halves_t(ket_r, ket))
        if has_cache:
            valid = band
        else:
            valid = band & ((ek >= SWA_WINDOW) | (sb * n_sub + a > 0))
        n_pairs = SWA_HEADS // 2

        def logits(j):
            s = _dot(qb_ref[pl.ds(r0, tq), 128 * j:128 * (j + 1)], k2t[(2 * j) // n_pairs])
            s = jnp.where(valid, s, NEG_INF)
            sinks = [sink_ref[2 * j + half] * LOG2_E for half in range(2)]
            ms = [jnp.maximum(jnp.max(s[:, half * SWA_KEYS:(half + 1) * SWA_KEYS], axis=-1, keepdims=True),
                              sinks[half]) for half in range(2)]
            return s, sinks, ms

        staged = [logits(j) for j in range(SWA_LOOKAHEAD)]
        for j in range(n_pairs):
            if j + SWA_LOOKAHEAD < n_pairs:
                staged.append(logits(j + SWA_LOOKAHEAD))
            s, sinks, ms = staged[j]
            ps, dens = [], []
            for half in range(2):
                p = jnp.exp2(s[:, half * SWA_KEYS:(half + 1) * SWA_KEYS] - ms[half])
                ps.append(p)
                dens.append(jnp.sum(p, axis=-1, keepdims=True) + jnp.exp2(sinks[half] - ms[half]))
            p = jnp.concatenate(ps, axis=1).astype(BF16)
            o = _dot(p, v2[(2 * j) // n_pairs])
            o = o / jnp.where(lo, dens[0], dens[1])
            bo_ref[pl.ds(r0, tq), 128 * j:128 * (j + 1)] = o.astype(BF16)

    for a in range(n_sub):
        sub_block(a)

    y = _dot(a_out.astype(BF16), wout_ref[0:POOL_W, :]) + _dot(bo_ref[...], wout_ref[POOL_W:, :])
    xo_ref[0] = x + y

    new_win = kv_ref[ts:ts + SWA_WINDOW, :]
    kv_ref[0:SWA_WINDOW, :] = new_win
    kv_o_ref[0] = new_win


def _even_mixer(x, g, w_in, pool_wbd, pool_scale, sink, w_out, cache_pool, cache_kv, *, pos0, has_cache, name):
    b, s, d = x.shape
    ts = _row_tile(s, 512)
    tq = min(ts, SWA_WINDOW)
    kv_rows = max(SWA_WINDOW + ts, (ts // tq - 1) * tq + SWA_KEYS)
    kern = functools.partial(_even_kernel, ts=ts, tq=tq, pos0=pos0, has_cache=has_cache)
    per_b = lambda r, w: pl.BlockSpec((1, r, w), lambda bi, si: (bi, 0, 0))
    return pl.pallas_call(
        kern,
        grid=(b, s // ts),
        in_specs=[
            pl.BlockSpec(memory_space=pltpu.SMEM),
            pl.BlockSpec((1, ts, d), lambda bi, si: (bi, si, 0)),
            _resident((1, d)), _resident(w_in.shape), _resident(pool_wbd.shape), _resident((1, POOL_W)),
            _resident(w_out.shape),
            per_b(POOL_HALO, POOL_W), per_b(SWA_WINDOW, 2 * SWA_KV_W),
        ],
        out_specs=[
            pl.BlockSpec((1, ts, d), lambda bi, si: (bi, si, 0)),
            per_b(POOL_HALO, POOL_W), per_b(SWA_WINDOW, 2 * SWA_KV_W),
        ],
        out_shape=[
            jax.ShapeDtypeStruct((b, s, d), F32),
            jax.ShapeDtypeStruct((b, POOL_HALO, POOL_W), F32),
            jax.ShapeDtypeStruct((b, SWA_WINDOW, 2 * SWA_KV_W), F32),
        ],
        scratch_shapes=[
            pltpu.VMEM((POOL_HALO + ts, POOL_W), F32),
            pltpu.VMEM((kv_rows, 2 * SWA_KV_W), F32),
            pltpu.VMEM((ts, SWA_Q_W), BF16),
            pltpu.VMEM((ts, SWA_Q_W), BF16),
        ],
        compiler_params=pltpu.CompilerParams(dimension_semantics=("arbitrary", "arbitrary"),
                                             vmem_limit_bytes=V7X_VMEM_LIMIT_BYTES),
        name=name,
    )(sink, x, g.reshape(1, d), w_in, pool_wbd, pool_scale.reshape(1, POOL_W), w_out, cache_pool, cache_kv)


def _gelu(x):
    return 0.5 * x * (1.0 + lax.erf(x * math.sqrt(0.5)))


def _odd_proj_kernel(*refs, lc, with_vg):
    (x_ref, g_ref, w_ref, lng_ref, lnb_ref, wcat_ref, bm_ref,
     q_ref, k_ref, v_ref, kb_ref, vb_ref, d_ref) = refs[:13]
    vg_ref = refs[13] if with_vg else None
    tm = x_ref.shape[0]

    hb = _rms(x_ref[...], g_ref[...]).astype(BF16)
    q_ref[...] = (_dot(hb, w_ref[:, 0:DIFF_W]) * (SM_SCALE * LOG2_E)).astype(BF16)
    k = _dot(hb, w_ref[:, DIFF_W:2 * DIFF_W])
    k_ref[...] = k
    kb_ref[...] = k.astype(BF16)
    v = _dot(hb, w_ref[:, 2 * DIFF_W:3 * DIFF_W])
    v_ref[...] = v
    vb_ref[...] = v.astype(BF16)

    u = _gelu(_dot(hb, w_ref[:, 3 * DIFF_W:3 * DIFF_W + GMLP_W]))
    gv = _gelu(_dot(hb, w_ref[:, 3 * DIFF_W + GMLP_W:]))
    mu = jnp.mean(gv, axis=-1, keepdims=True)
    cen = gv - mu
    var = jnp.mean(cen * cen, axis=-1, keepdims=True)
    vg = cen * lax.rsqrt(var + EPS) * lng_ref[...] + lnb_ref[...]
    if with_vg:
        vg_ref[...] = vg

    t_i = lax.broadcasted_iota(jnp.int32, (lc, GMLP_GROUPS * lc), 0)
    s_i = lax.broadcasted_iota(jnp.int32, (lc, GMLP_GROUPS * lc), 1) % lc
    wcat = jnp.where(s_i <= t_i, wcat_ref[...], 0.0).astype(BF16)
    lane = lax.broadcasted_iota(jnp.int32, (1, GMLP_W), 1) // GMLP_GW
    for c in range(tm // lc):
        rows = slice(c * lc, (c + 1) * lc)
        vc = vg[rows]
        vstack = jnp.concatenate([jnp.where(lane == gi, vc, 0.0) for gi in range(GMLP_GROUPS)],
                                 axis=0).astype(BF16)
        mix = _dot(wcat, vstack) + bm_ref[...]
        d_ref[rows, :] = (u[rows] * mix).astype(BF16)


def _odd_proj(x, g, w_in, ln_g, ln_b, wcat, bias_m, *, lc, with_vg, name):
    t, d = x.shape
    tm = _row_tile(t, 512)
    assert tm % lc == 0
    row = lambda w: pl.BlockSpec((tm, w), lambda i: (i, 0))
    out_shape = [
        jax.ShapeDtypeStruct((t, DIFF_W), BF16),
        jax.ShapeDtypeStruct((t, DIFF_W), F32),
        jax.ShapeDtypeStruct((t, DIFF_W), F32),
        jax.ShapeDtypeStruct((t, DIFF_W), BF16),
        jax.ShapeDtypeStruct((t, DIFF_W), BF16),
        jax.ShapeDtypeStruct((t, GMLP_W), BF16),
    ]
    out_specs = [row(DIFF_W)] * 5 + [row(GMLP_W)]
    if with_vg:
        out_shape.append(jax.ShapeDtypeStruct((t, GMLP_W), F32))
        out_specs.append(row(GMLP_W))
    return pl.pallas_call(
        functools.partial(_odd_proj_kernel, lc=lc, with_vg=with_vg),
        grid=(t // tm,),
        in_specs=[row(d), _resident((1, d)), _resident(w_in.shape), _resident((1, GMLP_W)),
                  _resident((1, GMLP_W)), _resident(wcat.shape), _resident(bias_m.shape)],
        out_specs=out_specs,
        out_shape=out_shape,
        compiler_params=pltpu.CompilerParams(dimension_semantics=("arbitrary",),
                                             vmem_limit_bytes=V7X_VMEM_LIMIT_BYTES),
        name=name,
    )(x, g.reshape(1, d), w_in, ln_g.reshape(1, GMLP_W), ln_b.reshape(1, GMLP_W), wcat, bias_m)


def _head_lanes(g):
    return slice(g * DIFF_DH, (g + 1) * DIFF_DH)


def _diff_lambda(lam_ref, lam_init):
    lp = lam_ref[...]
    return (jnp.exp(jnp.sum(lp[0:1] * lp[1:2], axis=-1, keepdims=True))
            - jnp.exp(jnp.sum(lp[2:3] * lp[3:4], axis=-1, keepdims=True)) + lam_init)


def _diff_diag_mask(tq):
    return (lax.broadcasted_iota(jnp.int32, (tq, 2 * tq), 0) // CHUNK
            <= (lax.broadcasted_iota(jnp.int32, (tq, 2 * tq), 1) % tq) // CHUNK)


def _diff_start(state, g, qi, tq):
    qs_ref, m_ref, acc_ref = state
    d_lo = lax.broadcasted_iota(jnp.int32, (DIFF_DH, 2 * tq), 0) < HEAD_DIM
    c_lo = lax.broadcasted_iota(jnp.int32, (DIFF_DH, 2 * tq), 1) < tq
    qt = jnp.concatenate([qi, qi], axis=0).astype(F32).T
    qs_ref[g] = jnp.where(d_lo == c_lo, qt, 0.0).astype(BF16)
    if m_ref is not None:
        m_ref[g] = jnp.full((1, 2 * tq), NEG_INF, F32)
    acc_ref[g] = jnp.zeros((DIFF_ACC_ROWS, 2 * tq), F32)


def _diff_scores(state, g, kb):
    return _dot(kb, state[0][g])


def _diff_update(state, g, s, vt, mask):
    _, m_ref, acc_ref = state
    if mask is not None:
        s = jnp.where(mask, s, NEG_INF)
    m_prev = m_ref[g]
    m_new = jnp.maximum(m_prev, jnp.max(s, axis=0, keepdims=True))
    alpha = jnp.exp2(m_prev - m_new)
    p = jnp.exp2(s - m_new)
    acc_ref[g] = alpha * acc_ref[g] + _dot(vt, p.astype(BF16))
    m_ref[g] = m_new


def _diff_heads_step(state, n_heads, s_first, key_block, value_block, mask, s_after):
    s_cur, s_nxt = s_first, None
    for g in range(n_heads):
        if g + 1 < n_heads:
            s_nxt = _diff_scores(state, g + 1, key_block(g + 1))
        elif s_after is not None:
            s_nxt = s_after()
        _diff_update(state, g, s_cur, value_block(g), mask)
        s_cur = s_nxt
    return s_nxt


def _diff_finish(state, g, lam, sg, tq, lam_init):
    acc = state[-1][g]
    ot = acc[:DIFF_DH] / acc[DIFF_DH:DIFF_DH + 1]
    dt = ot - lam * pltpu.roll(ot, tq, 1)
    dt = dt * lax.rsqrt(jnp.mean(dt * dt, axis=0, keepdims=True) + EPS) * sg
    return ((dt * (1.0 - lam_init)).T[:tq]).astype(BF16)


def _vt(v):
    keys = v.shape[0]
    ones_row = lax.broadcasted_iota(jnp.int32, (DIFF_ACC_ROWS - DIFF_DH, keys), 0) == 0
    return jnp.concatenate([v.astype(F32).T.astype(BF16), jnp.where(ones_row, 1.0, 0.0).astype(BF16)], axis=0)


def _diff_attn_kernel(lam_ref, sg_ref, q_ref, k_ref, v_ref, o_ref, vt_ref, s_ref, m_ref, a_ref,
                      qs_ref, acc_ref, *, tq, n_heads, lam_init):
    nq = q_ref.shape[1] // tq
    heads = range(n_heads)
    state = (qs_ref, None, acc_ref)
    lam = _diff_lambda(lam_ref, lam_init)
    diag_ok = _diff_diag_mask(tq)

    for g in heads:
        for c in range(nq):
            vt_ref[g, c] = _vt(v_ref[0, c * tq:(c + 1) * tq, _head_lanes(g)])

    def logits_phase(g, r0, src, dst, mask):
        s = _dot(k_ref[0, pl.ds(r0, tq), _head_lanes(g)], qs_ref[g])
        if mask is not None:
            s = jnp.where(mask, s, NEG_INF)
        s_ref[dst, g] = s
        m_old = m_ref[src, g]
        m_new = jnp.maximum(m_old, jnp.max(s, axis=0, keepdims=True))
        m_ref[dst, g] = m_new
        a_ref[dst, g] = jnp.exp2(m_old - m_new)

    def accum_phase(g, blk, slot):
        p = jnp.exp2(s_ref[slot, g] - m_ref[slot, g])
        acc_ref[g] = a_ref[slot, g] * acc_ref[g] + _dot(vt_ref[g, blk], p.astype(BF16))

    def step(blk, slot, mask):
        r1 = (blk + 1) * tq if isinstance(blk, int) else pl.multiple_of((blk + 1) * tq, tq)
        logits_phase(0, r1, slot, 1 - slot, mask)
        for g in heads:
            if g + 1 < n_heads:
                logits_phase(g + 1, r1, slot, 1 - slot, mask)
            accum_phase(g, blk, slot)

    def two_steps(t, c):
        step(2 * t, 0, None)
        step(2 * t + 1, 1, None)
        return c

    def q_block(i, odd):
        q0 = i * tq if isinstance(i, int) else pl.multiple_of(i * tq, tq)
        for g in heads:
            _diff_start(state, g, q_ref[0, pl.ds(q0, tq), _head_lanes(g)], tq)
            m_ref[1, g] = jnp.full((1, 2 * tq), NEG_INF, F32)
        if isinstance(i, int) and i == 0:
            for g in heads:
                logits_phase(g, 0, 1, 0, diag_ok)
            last_slot = 0
        else:
            for g in heads:
                logits_phase(g, 0, 1, 0, None)
            lax.fori_loop(0, (i - 1) // 2, two_steps, 0)
            if odd:
                step(i - 1, 0, diag_ok)
                last_slot = 1
            else:
                step(i - 2, 0, None)
                step(i - 1, 1, diag_ok)
                last_slot = 0
        for g in heads:
            accum_phase(g, i, last_slot)
        for g in heads:
            o_ref[0, pl.ds(q0, tq), _head_lanes(g)] = _diff_finish(state, g, lam, sg_ref[...], tq, lam_init)

    q_block(0, False)

    def q_pair(u, c):
        q_block(2 * u + 1, True)
        q_block(2 * u + 2, False)
        return c
    lax.fori_loop(0, (nq - 1) // 2, q_pair, 0)
    if nq % 2 == 0:
        q_block(nq - 1, True)


def _diff_attn_cached_kernel(lam_ref, sg_ref, q_ref, k_ref, v_ref, ck_ref, cv_ref, o_ref, *state, tq, lam_init):
    c = pl.program_id(1)
    heads = range(DIFF_HEADS)

    @pl.when(c == 0)
    def _():
        for g in heads:
            _diff_start(state, g, q_ref[0, :, _head_lanes(g)], tq)

    tkc = ck_ref.shape[0] // DIFF_HEADS
    head_rows = lambda g: pl.ds(g, tkc, stride=DIFF_HEADS)
    cached_keys = lambda g: ck_ref[head_rows(g), :].astype(BF16)
    _diff_heads_step(state, DIFF_HEADS, _diff_scores(state, 0, cached_keys(0)), cached_keys,
                     lambda g: _vt(cv_ref[head_rows(g), :]), None, None)

    @pl.when(c == pl.num_programs(1) - 1)
    def _():
        lam = _diff_lambda(lam_ref, lam_init)
        new_keys = lambda g: k_ref[0, :, _head_lanes(g)]
        _diff_heads_step(state, DIFF_HEADS, _diff_scores(state, 0, new_keys(0)), new_keys,
                         lambda g: _vt(v_ref[0, :, _head_lanes(g)]), _diff_diag_mask(tq), None)
        for g in heads:
            o_ref[0, :, _head_lanes(g)] = _diff_finish(state, g, lam, sg_ref[...], tq, lam_init)


def _diff_state_scratch(n_heads, tq):
    return [pltpu.VMEM((n_heads, DIFF_DH, 2 * tq), BF16),
            pltpu.VMEM((n_heads, 1, 2 * tq), F32),
            pltpu.VMEM((n_heads, DIFF_ACC_ROWS, 2 * tq), F32)]


def _diff_attn(q, kb, vb, lam_p, subln_g, *, lam_init, name):
    b, s, _ = q.shape
    tq = _row_tile(s, 256)
    n_heads = 4
    grp = pl.BlockSpec((1, s, n_heads * DIFF_DH), lambda bi, hi: (bi, 0, hi))
    return pl.pallas_call(
        functools.partial(_diff_attn_kernel, tq=tq, n_heads=n_heads, lam_init=lam_init),
        grid=(b, DIFF_HEADS // n_heads),
        in_specs=[_resident(lam_p.shape), _resident((DIFF_DH, 1)), grp, grp, grp],
        out_specs=grp,
        out_shape=jax.ShapeDtypeStruct((b, s, DIFF_W), BF16),
        scratch_shapes=[pltpu.VMEM((n_heads, s // tq, DIFF_ACC_ROWS, tq), BF16),
                        pltpu.VMEM((2, n_heads, tq, 2 * tq), F32),
                        pltpu.VMEM((2, n_heads, 1, 2 * tq), F32),
                        pltpu.VMEM((2, n_heads, 1, 2 * tq), F32),
                        pltpu.VMEM((n_heads, DIFF_DH, 2 * tq), BF16),
                        pltpu.VMEM((n_heads, DIFF_ACC_ROWS, 2 * tq), F32)],
        compiler_params=pltpu.CompilerParams(dimension_semantics=("arbitrary", "arbitrary"),
                                             vmem_limit_bytes=V7X_VMEM_LIMIT_BYTES),
        name=name,
    )(lam_p, subln_g.reshape(DIFF_DH, 1), q, kb, vb)


def _diff_attn_cached(q, kb, vb, cache_k, cache_v, layer_j, lam_p, subln_g, *, lam_init, name):
    b, s, _ = q.shape
    assert s % CHUNK == 0 and s <= 256, s
    n_layers, _, lc = cache_k.shape[:3]
    tkc = _row_tile(lc, 1024)
    cache_k, cache_v = (c.reshape(n_layers, b, lc * DIFF_HEADS, DIFF_DH) for c in (cache_k, cache_v))
    new = pl.BlockSpec((1, s, DIFF_W), lambda bi, ci: (bi, 0, 0))
    cache = pl.BlockSpec((None, None, tkc * DIFF_HEADS, DIFF_DH), lambda bi, ci: (layer_j, bi, ci, 0))
    return pl.pallas_call(
        functools.partial(_diff_attn_cached_kernel, tq=s, lam_init=lam_init),
        grid=(b, lc // tkc),
        in_specs=[_resident(lam_p.shape), _resident((DIFF_DH, 1)), new, new, new, cache, cache],
        out_specs=new,
        out_shape=jax.ShapeDtypeStruct((b, s, DIFF_W), BF16),
        scratch_shapes=_diff_state_scratch(DIFF_HEADS, s),
        compiler_params=pltpu.CompilerParams(dimension_semantics=("arbitrary", "arbitrary"),
                                             vmem_limit_bytes=V7X_VMEM_LIMIT_BYTES),
        name=name,
    )(lam_p, subln_g.reshape(DIFF_DH, 1), q, kb, vb, cache_k, cache_v)


def _block_diag(w):
    g, a, bb = w.shape
    eye = jnp.eye(g, dtype=w.dtype)
    return (eye[:, None, :, None] * w[:, :, None, :]).reshape(g * a, g * bb)


def _forward(xs, caches, wts):
    cache_pool, cache_swa_k, cache_swa_v, cache_diff_k, cache_diff_v = caches
    shapes = [x.shape for x in xs]
    d = shapes[0][2]
    norm_g = wts["norm_g"]
    depth = norm_g.shape[0]
    assert depth % 2 == 0
    past_len = cache_diff_k.shape[2]
    tags = ("p", "s")
    xt = [x.reshape(-1, d) for x in xs]
    outs = [dict(pool=[], swa_k=[], swa_v=[], diff_k=[], diff_v=[], gmlp_v=[]) for _ in xs]

    for l in range(depth):
        j = l // 2
        xt = _ffn(xt, norm_g[l, 0], wts["ffn"], (l, 0), name=f"ffn{l}a")
        if l % 2 == 0:
            for si, (b, s, _) in enumerate(shapes):
                prompt = si == 0
                if prompt:
                    cpool = jnp.zeros((b, POOL_HALO, POOL_W), F32)
                    ckv = jnp.zeros((b, SWA_WINDOW, 2 * SWA_KV_W), F32)
                else:
                    cpool = jnp.pad(cache_pool[j], ((0, 0), (POOL_HALO - cache_pool.shape[2], 0), (0, 0)))
                    ckv = jnp.concatenate([cache_swa_k[j].reshape(b, SWA_WINDOW, SWA_KV_W),
                                           cache_swa_v[j].reshape(b, SWA_WINDOW, SWA_KV_W)], axis=-1)
                x3, pool_o, kv_o = _even_mixer(
                    xt[si].reshape(b, s, d), norm_g[l, 1], wts["even_w_in"][j], wts["pool_wbd"][j],
                    wts["pool_scale"][j], wts["swa_sink"][j], wts["even_w_out"][j], cpool, ckv,
                    pos0=0 if prompt else past_len, has_cache=not prompt, name=f"{tags[si]}_even{l}")
                xt[si] = x3.reshape(b * s, d)
                outs[si]["pool"].append(pool_o[:, 1:])
                outs[si]["swa_k"].append(kv_o[..., :SWA_KV_W].reshape(b, SWA_WINDOW, 2, HEAD_DIM))
                outs[si]["swa_v"].append(kv_o[..., SWA_KV_W:].reshape(b, SWA_WINDOW, 2, HEAD_DIM))
            xt = _ffn(xt, norm_g[l, 2], wts["ffn"], (l, 1), name=f"ffn{l}b")
        else:
            lam_init = 0.8 - 0.6 * math.exp(-0.3 * l)
            mixes = []
            for si, (b, s, _) in enumerate(shapes):
                prompt = si == 0
                lc = 128 if prompt else s
                res = _odd_proj(xt[si], norm_g[l, 1], wts["odd_w_in"][j], wts["gmlp_ln_g"][j],
                                wts["gmlp_ln_b"][j], wts["gmlp_wcat"][(j, lc)], wts["gmlp_bias"][(j, lc)],
                                lc=lc, with_vg=not prompt, name=f"{tags[si]}_oddproj{l}")
                q, k, v, kb, vb, dd = res[:6]
                outs[si]["diff_k"].append(k.reshape(b, s, DIFF_HEADS, DIFF_DH))
                outs[si]["diff_v"].append(v.reshape(b, s, DIFF_HEADS, DIFF_DH))
                if not prompt:
                    outs[si]["gmlp_v"].append(res[6].reshape(b, s, GMLP_W))
                qkv = (q.reshape(b, s, DIFF_W), kb.reshape(b, s, DIFF_W), vb.reshape(b, s, DIFF_W))
                if prompt:
                    o = _diff_attn(*qkv, wts["diff_lambda"][j], wts["diff_subln_g"][j],
                                   lam_init=lam_init, name=f"{tags[si]}_diffattn{l}")
                else:
                    o = _diff_attn_cached(*qkv, cache_diff_k, cache_diff_v, j, wts["diff_lambda"][j],
                                          wts["diff_subln_g"][j], lam_init=lam_init,
                                          name=f"{tags[si]}_diffattn{l}")
                mixes.append((o.reshape(b * s, DIFF_W), dd))
            xt = _ffn(xt, norm_g[l, 2], wts["ffn"], (l, 1), mixes=mixes, w_out=wts["odd_w_out"][j],
                      final_g=wts["final_g"] if l == depth - 1 else None, name=f"ffn{l}b")
    return [x.reshape(shape) for x, shape in zip(xt, shapes)], outs


def _prep_weights(norm_g, final_g, ffn_gate, ffn_up, ffn_down, even_w_in, even_w_out, pool_w, pool_scale,
                  swa_sink, odd_w_in, odd_w_out, diff_lambda, diff_subln_g, gmlp_ln_g, gmlp_ln_b,
                  gmlp_w_s, gmlp_b_s, gmlp_chunks):
    ffn = (ffn_gate.astype(BF16), ffn_up.astype(BF16), ffn_down.astype(BF16))
    n_even, n_odd = even_w_in.shape[0], odd_w_in.shape[0]
    wcat, bias = {}, {}
    for j in range(n_odd):
        for lc in gmlp_chunks:
            ws = gmlp_w_s[j][:, :lc, :lc]
            wcat[(j, lc)] = jnp.transpose(ws, (1, 0, 2)).reshape(lc, GMLP_GROUPS * lc)
            bias[(j, lc)] = jnp.repeat(gmlp_b_s[j][:, :lc].T, GMLP_GW, axis=1)
    return dict(
        norm_g=norm_g, final_g=final_g, ffn=ffn,
        even_w_in=[even_w_in[j].astype(BF16) for j in range(n_even)],
        even_w_out=[even_w_out[j].astype(BF16) for j in range(n_even)],
        pool_wbd=[_block_diag(pool_w[j]).astype(BF16) for j in range(n_even)],
        pool_scale=pool_scale, swa_sink=swa_sink,
        odd_w_in=[odd_w_in[j].astype(BF16) for j in range(n_odd)],
        odd_w_out=[odd_w_out[j].astype(BF16) for j in range(n_odd)],
        diff_lambda=diff_lambda, diff_subln_g=diff_subln_g, gmlp_ln_g=gmlp_ln_g, gmlp_ln_b=gmlp_ln_b,
        gmlp_wcat=wcat, gmlp_bias=bias,
    )


def kernel(x_prompt, x_sample, cache_pool, cache_swa_k, cache_swa_v, cache_diff_k, cache_diff_v, norm_g, final_g, ffn_gate, ffn_up, ffn_down, even_w_in, even_w_out, pool_w, pool_scale, swa_sink, odd_w_in, odd_w_out, diff_lambda, diff_subln_g, gmlp_ln_g, gmlp_ln_b, gmlp_w_s, gmlp_b_s):
    wts = _prep_weights(norm_g, final_g, ffn_gate, ffn_up, ffn_down, even_w_in, even_w_out, pool_w,
                        pool_scale, swa_sink, odd_w_in, odd_w_out, diff_lambda, diff_subln_g, gmlp_ln_g,
                        gmlp_ln_b, gmlp_w_s, gmlp_b_s, gmlp_chunks=(128, x_sample.shape[1]))
    caches = (cache_pool, cache_swa_k, cache_swa_v, cache_diff_k, cache_diff_v)
    (y_p, y_s), (out_p, out_s) = _forward((x_prompt, x_sample), caches, wts)
    return (y_p, y_s,
            jnp.stack(out_p["pool"]), jnp.stack(out_s["pool"]),
            jnp.stack(out_p["swa_k"]), jnp.stack(out_s["swa_k"]),
            jnp.stack(out_p["swa_v"]), jnp.stack(out_s["swa_v"]),
            jnp.stack(out_p["diff_k"]), jnp.stack(out_s["diff_k"]),
            jnp.stack(out_p["diff_v"]), jnp.stack(out_s["diff_v"]),
            jnp.stack(out_s["gmlp_v"]))
```

```python
import functools
import math

import jax
import jax.numpy as jnp
from jax import lax
from jax.experimental import pallas as pl
from jax.experimental.pallas import tpu as pltpu

F32 = jnp.float32
BF16 = jnp.bfloat16

EPS = 1e-6
CHUNK = 64
HEAD_DIM = 64
POOL_WINDOWS = (2, 4, 8, 16)
POOL_GW = 96
POOL_W = 4 * POOL_GW
POOL_HALO = 16
SWA_WINDOW = 128
SWA_KEYS = 256
SWA_HEADS = 16
SWA_LOOKAHEAD = 4
SWA_Q_W = SWA_HEADS * HEAD_DIM
SWA_KV_W = 2 * HEAD_DIM
DIFF_HEADS = 8
DIFF_DH = 2 * HEAD_DIM
DIFF_W = DIFF_HEADS * DIFF_DH
DIFF_ACC_ROWS = DIFF_DH + 16
GMLP_GROUPS = 4
GMLP_GW = 96
GMLP_W = GMLP_GROUPS * GMLP_GW
SM_SCALE = HEAD_DIM ** -0.5
LOG2_E = math.log2(math.e)

V7X_VMEM_LIMIT_BYTES = 56 * 1024 * 1024
NEG_INF = float("-inf")


def _rms(x, g):
    return x * lax.rsqrt(jnp.mean(x * x, axis=-1, keepdims=True) + EPS) * g


def _dot(a, b):
    return jnp.dot(a, b, preferred_element_type=F32)


def _dot_nt(a, b):
    return lax.dot_general(a, b, (((1,), (1,)), ((), ())), preferred_element_type=F32)


def _resident(shape):
    nd = len(shape)
    return pl.BlockSpec(shape, lambda *_: (0,) * nd, pipeline_mode=pl.Buffered(1))


def _row_tile(n_rows, want):
    t = min(want, n_rows)
    assert n_rows % t == 0, (n_rows, t)
    return t


def _ffn_kernel(*refs, ff_chunk, steps, has_mix, has_final):
    n = len(steps)
    it = iter(refs)
    x_refs = [next(it) for _ in range(n)]
    if has_mix:
        o_refs = [next(it) for _ in range(n)]
        d_refs = [next(it) for _ in range(n)]
        wo_ref = next(it)
    g_ref, wg_ref, wu_ref, wd_ref = next(it), next(it), next(it), next(it)
    if has_final:
        fg_ref = next(it)
    out_refs = [next(it) for _ in range(n)]
    acc_ref = next(it)

    def run(s):
        x = x_refs[s][...]
        if has_mix:
            n_o = o_refs[s].shape[1]
            x = x + _dot(o_refs[s][...], wo_ref[0:n_o, :]) + _dot(d_refs[s][...], wo_ref[n_o:, :])
        hb = _rms(x, g_ref[...]).astype(BF16)
        d_ff = wg_ref.shape[1]
        for c in range(d_ff // ff_chunk):
            sl = slice(c * ff_chunk, (c + 1) * ff_chunk)
            gate = _dot(hb, wg_ref[:, sl])
            up = _dot(hb, wu_ref[:, sl])
            act = (gate * jax.nn.sigmoid(gate) * up).astype(BF16)
            part = _dot(act, wd_ref[sl, :])
            if c == 0:
                acc_ref[...] = part
            else:
                acc_ref[...] += part
        y = x + 0.5 * acc_ref[...]
        if has_final:
            y = _rms(y, fg_ref[...])
        out_refs[s][...] = y

    i = pl.program_id(0)
    first = 0
    for s in range(n):
        pl.when((i >= first) & (i < first + steps[s]))(functools.partial(run, s))
        first += steps[s]


def _stacked(w, idx):
    k = len(idx)
    return pl.BlockSpec((None,) * k + tuple(w.shape[k:]), lambda *_: tuple(idx) + (0,) * (w.ndim - k),
                        pipeline_mode=pl.Buffered(1))


def _ffn(xs, g, ffn_w, layer, mixes=None, w_out=None, final_g=None, *, name):
    d = xs[0].shape[1]
    wg, wu, wd = ffn_w
    d_ff = wg.shape[-1]
    ff_chunk = 256
    assert d_ff % ff_chunk == 0
    tm = _row_tile(min(x.shape[0] for x in xs), 512)
    steps = [x.shape[0] // tm for x in xs]
    starts = [sum(steps[:s]) for s in range(len(xs))]

    def rows(s, w):
        return pl.BlockSpec((tm, w), lambda i: (jnp.clip(i - starts[s], 0, steps[s] - 1), 0))

    args, specs = list(xs), [rows(s, d) for s in range(len(xs))]
    if mixes is not None:
        for k in range(2):
            args += [m[k] for m in mixes]
            specs += [rows(s, m[k].shape[1]) for s, m in enumerate(mixes)]
        args.append(w_out)
        specs.append(_resident(w_out.shape))
    args += [g.reshape(1, d), wg, wu, wd]
    specs += [_resident((1, d)), _stacked(wg, layer), _stacked(wu, layer), _stacked(wd, layer)]
    if final_g is not None:
        args.append(final_g.reshape(1, d))
        specs.append(_resident((1, d)))
    kern = functools.partial(_ffn_kernel, ff_chunk=ff_chunk, steps=tuple(steps), has_mix=mixes is not None,
                             has_final=final_g is not None)
    return list(pl.pallas_call(
        kern,
        grid=(sum(steps),),
        in_specs=specs,
        out_specs=[rows(s, d) for s in range(len(xs))],
        out_shape=[jax.ShapeDtypeStruct(x.shape, F32) for x in xs],
        scratch_shapes=[pltpu.VMEM((tm, d), F32)],
        compiler_params=pltpu.CompilerParams(dimension_semantics=("arbitrary",),
                                             vmem_limit_bytes=V7X_VMEM_LIMIT_BYTES),
        name=name,
    )(*args))


def _even_kernel(sink_ref, x_ref, g_ref, win_ref, pw_ref, ps_ref, wout_ref, cpool_ref, ckv_ref,
                 xo_ref, pool_o_ref, kv_o_ref,
                 e_ref, kv_ref, qb_ref, bo_ref, *, ts, tq, pos0, has_cache):
    sb = pl.program_id(1)
    n_sub = ts // tq
    kv_rows = kv_ref.shape[0]

    @pl.when(sb == 0)
    def _():
        e_ref[0:POOL_HALO, :] = cpool_ref[0]
        kv_ref[0:SWA_WINDOW, :] = ckv_ref[0]
        if kv_rows > SWA_WINDOW + ts:
            kv_ref[SWA_WINDOW + ts:, :] = jnp.zeros((kv_rows - SWA_WINDOW - ts, 2 * SWA_KV_W), F32)

    x = x_ref[0]
    hb = _rms(x, g_ref[...]).astype(BF16)
    u = _dot(hb, win_ref[:, 0:POOL_W])
    qb_ref[...] = (_dot(hb, win_ref[:, POOL_W:POOL_W + SWA_Q_W]) * (SM_SCALE * LOG2_E)).astype(BF16)
    kv_ref[SWA_WINDOW:SWA_WINDOW + ts, :] = _dot(hb, win_ref[:, POOL_W + SWA_Q_W:])
    e_ref[POOL_HALO:POOL_HALO + ts, :] = u

    ev = e_ref[...]
    s2 = ev + pltpu.roll(ev, 1, 0)
    s4 = s2 + pltpu.roll(s2, 2, 0)
    s8 = s4 + pltpu.roll(s4, 4, 0)
    s16 = s8 + pltpu.roll(s8, 8, 0)
    lane = lax.broadcasted_iota(jnp.int32, (1, POOL_W), 1)
    g0, g1, g2 = lane < POOL_GW, lane < 2 * POOL_GW, lane < 3 * POOL_GW
    sw = jnp.where(g0, s2, jnp.where(g1, s4, jnp.where(g2, s8, s16)))[POOL_HALO:]
    win = jnp.where(g0, POOL_WINDOWS[0], jnp.where(g1, POOL_WINDOWS[1],
                                                  jnp.where(g2, POOL_WINDOWS[2], POOL_WINDOWS[3])))
    pos = pos0 + sb * ts + lax.broadcasted_iota(jnp.int32, (ts, 1), 0)
    cnt = jnp.minimum(pos + 1, win).astype(F32)
    diff = sw / cnt - u
    a_out = _dot(diff.astype(BF16), pw_ref[...]) * ps_ref[...]
    new_halo = ev[ts:ts + POOL_HALO]
    e_ref[0:POOL_HALO, :] = new_halo
    pool_o_ref[0] = new_halo

    lane128 = lax.broadcasted_iota(jnp.int32, (1, 2 * HEAD_DIM), 1)
    lo = lane128 < HEAD_DIM
    rq = lax.broadcasted_iota(jnp.int32, (tq, 2 * SWA_KEYS), 0) // CHUNK
    col = lax.broadcasted_iota(jnp.int32, (tq, 2 * SWA_KEYS), 1)
    ek = jnp.where(col >= SWA_KEYS, col - SWA_KEYS, col)
    ekc = ek // CHUNK
    band = (rq <= ekc) & (ekc <= rq + SWA_WINDOW // CHUNK)

    def halves(xa, xb):
        return jnp.concatenate([jnp.where(lo, xa, 0.0), jnp.where(lo, 0.0, xb)], axis=0).astype(BF16)

    row_lo = lax.broadcasted_iota(jnp.int32, (2 * HEAD_DIM, 1), 0) < HEAD_DIM

    def halves_t(xa, xb):
        return jnp.concatenate([jnp.where(row_lo, xa, 0.0), jnp.where(row_lo, 0.0, xb)], axis=1).astype(BF16)

    def sub_block(a):
        r0 = a * tq
        kve = kv_ref[pl.ds(r0, SWA_KEYS), :]
        ke, ve = kve[:, :SWA_KV_W], kve[:, SWA_KV_W:]
        ve_r = pltpu.roll(ve, HEAD_DIM, 1)
        v2 = (halves(ve, ve_r), halves(ve_r, ve))
        ket = ke.T
        ket_r = pltpu.roll(ket, HEAD_DIM, 0)
        k2t = (halves_t(ket, ket_r), halves_t(ket_r, ket))
        if has_cache:
            valid = band
        else:
            valid = band & ((ek >= SWA_WINDOW) | (sb * n_sub + a > 0))
        n_pairs = SWA_HEADS // 2

        def logits(j):
            s = _dot(qb_ref[pl.ds(r0, tq), 128 * j:128 * (j + 1)], k2t[(2 * j) // n_pairs])
            s = jnp.where(valid, s, NEG_INF)
            sinks = [sink_ref[2 * j + half] * LOG2_E for half in range(2)]
            ms = [jnp.maximum(jnp.max(s[:, half * SWA_KEYS:(half + 1) * SWA_KEYS], axis=-1, keepdims=True),
                              sinks[half]) for half in range(2)]
            return s, sinks, ms

        staged = [logits(j) for j in range(SWA_LOOKAHEAD)]
        for j in range(n_pairs):
            if j + SWA_LOOKAHEAD < n_pairs:
                staged.append(logits(j + SWA_LOOKAHEAD))
            s, sinks, ms = staged[j]
            ps, dens = [], []
            for half in range(2):
                p = jnp.exp2(s[:, half * SWA_KEYS:(half + 1) * SWA_KEYS] - ms[half])
                ps.append(p)
                dens.append(jnp.sum(p, axis=-1, keepdims=True) + jnp.exp2(sinks[half] - ms[half]))
            p = jnp.concatenate(ps, axis=1).astype(BF16)
            o = _dot(p, v2[(2 * j) // n_pairs])
            o = o / jnp.where(lo, dens[0], dens[1])
            bo_ref[pl.ds(r0, tq), 128 * j:128 * (j + 1)] = o.astype(BF16)

    for a in range(n_sub):
        sub_block(a)

    y = _dot(a_out.astype(BF16), wout_ref[0:POOL_W, :]) + _dot(bo_ref[...], wout_ref[POOL_W:, :])
    xo_ref[0] = x + y

    new_win = kv_ref[ts:ts + SWA_WINDOW, :]
    kv_ref[0:SWA_WINDOW, :] = new_win
    kv_o_ref[0] = new_win


def _even_mixer(x, g, w_in, pool_wbd, pool_scale, sink, w_out, cache_pool, cache_kv, *, pos0, has_cache, name):
    b, s, d = x.shape
    ts = _row_tile(s, 512)
    tq = min(ts, SWA_WINDOW)
    kv_rows = max(SWA_WINDOW + ts, (ts // tq - 1) * tq + SWA_KEYS)
    kern = functools.partial(_even_kernel, ts=ts, tq=tq, pos0=pos0, has_cache=has_cache)
    per_b = lambda r, w: pl.BlockSpec((1, r, w), lambda bi, si: (bi, 0, 0))
    return pl.pallas_call(
        kern,
        grid=(b, s // ts),
        in_specs=[
            pl.BlockSpec(memory_space=pltpu.SMEM),
            pl.BlockSpec((1, ts, d), lambda bi, si: (bi, si, 0)),
            _resident((1, d)), _resident(w_in.shape), _resident(pool_wbd.shape), _resident((1, POOL_W)),
            _resident(w_out.shape),
            per_b(POOL_HALO, POOL_W), per_b(SWA_WINDOW, 2 * SWA_KV_W),
        ],
        out_specs=[
            pl.BlockSpec((1, ts, d), lambda bi, si: (bi, si, 0)),
            per_b(POOL_HALO, POOL_W), per_b(SWA_WINDOW, 2 * SWA_KV_W),
        ],
        out_shape=[
            jax.ShapeDtypeStruct((b, s, d), F32),
            jax.ShapeDtypeStruct((b, POOL_HALO, POOL_W), F32),
            jax.ShapeDtypeStruct((b, SWA_WINDOW, 2 * SWA_KV_W), F32),
        ],
        scratch_shapes=[
            pltpu.VMEM((POOL_HALO + ts, POOL_W), F32),
            pltpu.VMEM((kv_rows, 2 * SWA_KV_W), F32),
            pltpu.VMEM((ts, SWA_Q_W), BF16),
            pltpu.VMEM((ts, SWA_Q_W), BF16),
        ],
        compiler_params=pltpu.CompilerParams(dimension_semantics=("arbitrary", "arbitrary"),
                                             vmem_limit_bytes=V7X_VMEM_LIMIT_BYTES),
        name=name,
    )(sink, x, g.reshape(1, d), w_in, pool_wbd, pool_scale.reshape(1, POOL_W), w_out, cache_pool, cache_kv)


def _gelu(x):
    return 0.5 * x * (1.0 + lax.erf(x * math.sqrt(0.5)))


def _odd_proj_kernel(*refs, lc, with_vg):
    (x_ref, g_ref, w_ref, lng_ref, lnb_ref, wcat_ref, bm_ref,
     q_ref, k_ref, v_ref, kb_ref, vb_ref, d_ref) = refs[:13]
    vg_ref = refs[13] if with_vg else None
    tm = x_ref.shape[0]

    hb = _rms(x_ref[...], g_ref[...]).astype(BF16)
    q_ref[...] = (_dot(hb, w_ref[:, 0:DIFF_W]) * (SM_SCALE * LOG2_E)).astype(BF16)
    k = _dot(hb, w_ref[:, DIFF_W:2 * DIFF_W])
    k_ref[...] = k
    kb_ref[...] = k.astype(BF16)
    v = _dot(hb, w_ref[:, 2 * DIFF_W:3 * DIFF_W])
    v_ref[...] = v
    vb_ref[...] = v.astype(BF16)

    u = _gelu(_dot(hb, w_ref[:, 3 * DIFF_W:3 * DIFF_W + GMLP_W]))
    gv = _gelu(_dot(hb, w_ref[:, 3 * DIFF_W + GMLP_W:]))
    mu = jnp.mean(gv, axis=-1, keepdims=True)
    cen = gv - mu
    var = jnp.mean(cen * cen, axis=-1, keepdims=True)
    vg = cen * lax.rsqrt(var + EPS) * lng_ref[...] + lnb_ref[...]
    if with_vg:
        vg_ref[...] = vg

    t_i = lax.broadcasted_iota(jnp.int32, (lc, GMLP_GROUPS * lc), 0)
    s_i = lax.broadcasted_iota(jnp.int32, (lc, GMLP_GROUPS * lc), 1) % lc
    wcat = jnp.where(s_i <= t_i, wcat_ref[...], 0.0).astype(BF16)
    lane = lax.broadcasted_iota(jnp.int32, (1, GMLP_W), 1) // GMLP_GW
    for c in range(tm // lc):
        rows = slice(c * lc, (c + 1) * lc)
        vc = vg[rows]
        vstack = jnp.concatenate([jnp.where(lane == gi, vc, 0.0) for gi in range(GMLP_GROUPS)],
                                 axis=0).astype(BF16)
        mix = _dot(wcat, vstack) + bm_ref[...]
        d_ref[rows, :] = (u[rows] * mix).astype(BF16)


def _odd_proj(x, g, w_in, ln_g, ln_b, wcat, bias_m, *, lc, with_vg, name):
    t, d = x.shape
    tm = _row_tile(t, 512)
    assert tm % lc == 0
    row = lambda w: pl.BlockSpec((tm, w), lambda i: (i, 0))
    out_shape = [
        jax.ShapeDtypeStruct((t, DIFF_W), BF16),
        jax.ShapeDtypeStruct((t, DIFF_W), F32),
        jax.ShapeDtypeStruct((t, DIFF_W), F32),
        jax.ShapeDtypeStruct((t, DIFF_W), BF16),
        jax.ShapeDtypeStruct((t, DIFF_W), BF16),
        jax.ShapeDtypeStruct((t, GMLP_W), BF16),
    ]
    out_specs = [row(DIFF_W)] * 5 + [row(GMLP_W)]
    if with_vg:
        out_shape.append(jax.ShapeDtypeStruct((t, GMLP_W), F32))
        out_specs.append(row(GMLP_W))
    return pl.pallas_call(
        functools.partial(_odd_proj_kernel, lc=lc, with_vg=with_vg),
        grid=(t // tm,),
        in_specs=[row(d), _resident((1, d)), _resident(w_in.shape), _resident((1, GMLP_W)),
                  _resident((1, GMLP_W)), _resident(wcat.shape), _resident(bias_m.shape)],
        out_specs=out_specs,
        out_shape=out_shape,
        compiler_params=pltpu.CompilerParams(dimension_semantics=("arbitrary",),
                                             vmem_limit_bytes=V7X_VMEM_LIMIT_BYTES),
        name=name,
    )(x, g.reshape(1, d), w_in, ln_g.reshape(1, GMLP_W), ln_b.reshape(1, GMLP_W), wcat, bias_m)


def _head_lanes(g):
    return slice(g * DIFF_DH, (g + 1) * DIFF_DH)


def _diff_lambda(lam_ref, lam_init):
    lp = lam_ref[...]
    return (jnp.exp(jnp.sum(lp[0:1] * lp[1:2], axis=-1, keepdims=True))
            - jnp.exp(jnp.sum(lp[2:3] * lp[3:4], axis=-1, keepdims=True)) + lam_init)


def _diff_diag_mask(tq):
    return (lax.broadcasted_iota(jnp.int32, (tq, 2 * tq), 0) // CHUNK
            <= (lax.broadcasted_iota(jnp.int32, (tq, 2 * tq), 1) % tq) // CHUNK)


def _diff_stage_queries(qs_ref, g, qi, tq):
    d_lo = lax.broadcasted_iota(jnp.int32, (DIFF_DH, 2 * tq), 0) < HEAD_DIM
    c_lo = lax.broadcasted_iota(jnp.int32, (DIFF_DH, 2 * tq), 1) < tq
    qt = jnp.concatenate([qi, qi], axis=0).astype(F32).T
    qs_ref[g] = jnp.where(d_lo == c_lo, qt, 0.0).astype(BF16)


def _diff_start(state, g, qi, tq):
    qs_ref, m_ref, acc_ref = state
    _diff_stage_queries(qs_ref, g, qi, tq)
    m_ref[g] = jnp.full((1, 2 * tq), NEG_INF, F32)
    acc_ref[g] = jnp.zeros((DIFF_ACC_ROWS, 2 * tq), F32)


def _diff_scores(state, g, kb):
    return _dot(kb, state[0][g])


def _diff_update(state, g, s, vt, mask):
    _, m_ref, acc_ref = state
    if mask is not None:
        s = jnp.where(mask, s, NEG_INF)
    m_prev = m_ref[g]
    m_new = jnp.maximum(m_prev, jnp.max(s, axis=0, keepdims=True))
    alpha = jnp.exp2(m_prev - m_new)
    p = jnp.exp2(s - m_new)
    acc_ref[g] = alpha * acc_ref[g] + _dot(vt, p.astype(BF16))
    m_ref[g] = m_new


def _diff_heads_step(state, n_heads, s_first, key_block, value_block, mask, s_after):
    s_cur, s_nxt = s_first, None
    for g in range(n_heads):
        if g + 1 < n_heads:
            s_nxt = _diff_scores(state, g + 1, key_block(g + 1))
        elif s_after is not None:
            s_nxt = s_after()
        _diff_update(state, g, s_cur, value_block(g), mask)
        s_cur = s_nxt
    return s_nxt


def _diff_finish(state, g, lam, sg, tq, lam_init):
    acc = state[-1][g]
    ot = acc[:DIFF_DH] / acc[DIFF_DH:DIFF_DH + 1]
    dt = ot - lam * pltpu.roll(ot, tq, 1)
    dt = dt * lax.rsqrt(jnp.mean(dt * dt, axis=0, keepdims=True) + EPS) * sg
    return ((dt * (1.0 - lam_init)).T[:tq]).astype(BF16)


def _vt(v):
    keys = v.shape[0]
    ones_row = lax.broadcasted_iota(jnp.int32, (DIFF_ACC_ROWS - DIFF_DH, keys), 0) == 0
    return jnp.concatenate([v.astype(F32).T.astype(BF16), jnp.where(ones_row, 1.0, 0.0).astype(BF16)], axis=0)


def _diff_attn_kernel(lam_ref, sg_ref, q_ref, k_ref, v_ref, o_ref, vt_ref, s_ref, m_ref, a_ref,
                      qs_ref, acc_ref, *, tq, n_heads, lam_init):
    FIRST = 2
    nq = q_ref.shape[1] // tq
    heads = range(n_heads)
    state = (qs_ref, None, acc_ref)
    lam = _diff_lambda(lam_ref, lam_init)
    diag_ok = _diff_diag_mask(tq)

    for g in heads:
        for c in range(nq):
            vt_ref[g, c] = _vt(v_ref[0, c * tq:(c + 1) * tq, _head_lanes(g)])

    def rows(blk):
        return blk * tq if isinstance(blk, int) else pl.multiple_of(blk * tq, tq)

    def logits_phase(g, r0, src, dst, mask):
        s = _dot(k_ref[0, pl.ds(r0, tq), _head_lanes(g)], qs_ref[g])
        if mask is not None:
            s = jnp.where(mask, s, NEG_INF)
        s_ref[dst, g] = s
        col_max = jnp.max(s, axis=0, keepdims=True)
        if src is None:
            m_ref[dst, g] = col_max
        else:
            m_old = m_ref[src, g]
            m_new = jnp.maximum(m_old, col_max)
            m_ref[dst, g] = m_new
            a_ref[dst, g] = jnp.exp2(m_old - m_new)

    def accum_phase(g, blk, slot, first=False):
        p = jnp.exp2(s_ref[slot, g] - m_ref[slot, g])
        pv = _dot(vt_ref[g, blk], p.astype(BF16))
        acc_ref[g] = pv if first else a_ref[slot, g] * acc_ref[g] + pv

    def stage(i, mask):
        for g in heads:
            _diff_stage_queries(qs_ref, g, q_ref[0, pl.ds(rows(i), tq), _head_lanes(g)], tq)
            logits_phase(g, 0, None, FIRST, mask)

    def step(blk, slot, dst, mask):
        r1 = rows(blk + 1)
        logits_phase(0, r1, slot, dst, mask)
        for g in heads:
            if g + 1 < n_heads:
                logits_phase(g + 1, r1, slot, dst, mask)
            accum_phase(g, blk, slot, first=slot == FIRST)

    def two_steps(t, c):
        step(2 * t + 1, 0, 1, None)
        step(2 * t + 2, 1, 0, None)
        return c

    def q_block(i, odd):
        if isinstance(i, int) and i == 0:
            last_slot = FIRST
        elif odd == "one":
            step(0, FIRST, 0, diag_ok)
            last_slot = 0
        else:
            step(0, FIRST, 0, None)
            if odd:
                lax.fori_loop(0, (i - 3) // 2, two_steps, 0)
                step(i - 2, 0, 1, None)
                step(i - 1, 1, 0, diag_ok)
                last_slot = 0
            else:
                lax.fori_loop(0, (i - 2) // 2, two_steps, 0)
                step(i - 1, 0, 1, diag_ok)
                last_slot = 1
        nxt = min(i + 1, nq - 1) if isinstance(i, int) else jnp.minimum(i + 1, nq - 1)
        if last_slot == FIRST:
            for g in heads:
                accum_phase(g, i, last_slot, first=True)
            stage(nxt, None)
        else:
            stage(nxt, None)
            for g in heads:
                accum_phase(g, i, last_slot)
        for g in heads:
            o_ref[0, pl.ds(rows(i), tq), _head_lanes(g)] = _diff_finish(state, g, lam, sg_ref[...], tq, lam_init)

    stage(0, diag_ok)
    q_block(0, False)
    if nq > 1:
        def q_one(i, c):
            q_block(i, "one")
            return c
        lax.fori_loop(1, jnp.minimum(pl.num_programs(0) + pl.program_id(0), 2), q_one, 0)

    def q_pair(u, c):
        q_block(2 * u + 2, False)
        q_block(2 * u + 3, True)
        return c
    lax.fori_loop(0, (nq - 2) // 2, q_pair, 0)
    if nq > 2 and nq % 2 == 1:
        q_block(nq - 1, False)


def _diff_attn_cached_kernel(lam_ref, sg_ref, q_ref, k_ref, v_ref, ck_ref, cv_ref, o_ref, *state, tq, lam_init):
    c = pl.program_id(1)
    heads = range(DIFF_HEADS)

    @pl.when(c == 0)
    def _():
        for g in heads:
            _diff_start(state, g, q_ref[0, :, _head_lanes(g)], tq)

    tkc = ck_ref.shape[0] // DIFF_HEADS
    head_rows = lambda g: pl.ds(g, tkc, stride=DIFF_HEADS)
    cached_keys = lambda g: ck_ref[head_rows(g), :].astype(BF16)
    _diff_heads_step(state, DIFF_HEADS, _diff_scores(state, 0, cached_keys(0)), cached_keys,
                     lambda g: _vt(cv_ref[head_rows(g), :]), None, None)

    @pl.when(c == pl.num_programs(1) - 1)
    def _():
        lam = _diff_lambda(lam_ref, lam_init)
        new_keys = lambda g: k_ref[0, :, _head_lanes(g)]
        _diff_heads_step(state, DIFF_HEADS, _diff_scores(state, 0, new_keys(0)), new_keys,
                         lambda g: _vt(v_ref[0, :, _head_lanes(g)]), _diff_diag_mask(tq), None)
        for g in heads:
            o_ref[0, :, _head_lanes(g)] = _diff_finish(state, g, lam, sg_ref[...], tq, lam_init)


def _diff_state_scratch(n_heads, tq):
    return [pltpu.VMEM((n_heads, DIFF_DH, 2 * tq), BF16),
            pltpu.VMEM((n_heads, 1, 2 * tq), F32),
            pltpu.VMEM((n_heads, DIFF_ACC_ROWS, 2 * tq), F32)]


def _diff_attn(q, kb, vb, lam_p, subln_g, *, lam_init, name):
    b, s, _ = q.shape
    tq = _row_tile(s, 256)
    n_heads = 4
    grp = pl.BlockSpec((1, s, n_heads * DIFF_DH), lambda bi, hi: (bi, 0, hi))
    return pl.pallas_call(
        functools.partial(_diff_attn_kernel, tq=tq, n_heads=n_heads, lam_init=lam_init),
        grid=(b, DIFF_HEADS // n_heads),
        in_specs=[_resident(lam_p.shape), _resident((DIFF_DH, 1)), grp, grp, grp],
        out_specs=grp,
        out_shape=jax.ShapeDtypeStruct((b, s, DIFF_W), BF16),
        scratch_shapes=[pltpu.VMEM((n_heads, s // tq, DIFF_ACC_ROWS, tq), BF16),
                        pltpu.VMEM((3, n_heads, tq, 2 * tq), F32),
                        pltpu.VMEM((3, n_heads, 1, 2 * tq), F32),
                        pltpu.VMEM((3, n_heads, 1, 2 * tq), F32),
                        pltpu.VMEM((n_heads, DIFF_DH, 2 * tq), BF16),
                        pltpu.VMEM((n_heads, DIFF_ACC_ROWS, 2 * tq), F32)],
        compiler_params=pltpu.CompilerParams(dimension_semantics=("arbitrary", "arbitrary"),
                                             vmem_limit_bytes=V7X_VMEM_LIMIT_BYTES),
        name=name,
    )(lam_p, subln_g.reshape(DIFF_DH, 1), q, kb, vb)


def _diff_attn_cached(q, kb, vb, cache_k, cache_v, layer_j, lam_p, subln_g, *, lam_init, name):
    b, s, _ = q.shape
    assert s % CHUNK == 0 and s <= 256, s
    n_layers, _, lc = cache_k.shape[:3]
    tkc = _row_tile(lc, 1024)
    cache_k, cache_v = (c.reshape(n_layers, b, lc * DIFF_HEADS, DIFF_DH) for c in (cache_k, cache_v))
    new = pl.BlockSpec((1, s, DIFF_W), lambda bi, ci: (bi, 0, 0))
    cache = pl.BlockSpec((None, None, tkc * DIFF_HEADS, DIFF_DH), lambda bi, ci: (layer_j, bi, ci, 0))
    return pl.pallas_call(
        functools.partial(_diff_attn_cached_kernel, tq=s, lam_init=lam_init),
        grid=(b, lc // tkc),
        in_specs=[_resident(lam_p.shape), _resident((DIFF_DH, 1)), new, new, new, cache, cache],
        out_specs=new,
        out_shape=jax.ShapeDtypeStruct((b, s, DIFF_W), BF16),
        scratch_shapes=_diff_state_scratch(DIFF_HEADS, s),
        compiler_params=pltpu.CompilerParams(dimension_semantics=("arbitrary", "arbitrary"),
                                             vmem_limit_bytes=V7X_VMEM_LIMIT_BYTES),
        name=name,
    )(lam_p, subln_g.reshape(DIFF_DH, 1), q, kb, vb, cache_k, cache_v)


def _block_diag(w):
    g, a, bb = w.shape
    eye = jnp.eye(g, dtype=w.dtype)
    return (eye[:, None, :, None] * w[:, :, None, :]).reshape(g * a, g * bb)


def _forward(xs, caches, wts):
    cache_pool, cache_swa_k, cache_swa_v, cache_diff_k, cache_diff_v = caches
    shapes = [x.shape for x in xs]
    d = shapes[0][2]
    norm_g = wts["norm_g"]
    depth = norm_g.shape[0]
    assert depth % 2 == 0
    past_len = cache_diff_k.shape[2]
    tags = ("p", "s")
    xt = [x.reshape(-1, d) for x in xs]
    outs = [dict(pool=[], swa_k=[], swa_v=[], diff_k=[], diff_v=[], gmlp_v=[]) for _ in xs]

    for l in range(depth):
        j = l // 2
        xt = _ffn(xt, norm_g[l, 0], wts["ffn"], (l, 0), name=f"ffn{l}a")
        if l % 2 == 0:
            for si, (b, s, _) in enumerate(shapes):
                prompt = si == 0
                if prompt:
                    cpool = jnp.zeros((b, POOL_HALO, POOL_W), F32)
                    ckv = jnp.zeros((b, SWA_WINDOW, 2 * SWA_KV_W), F32)
                else:
                    cpool = jnp.pad(cache_pool[j], ((0, 0), (POOL_HALO - cache_pool.shape[2], 0), (0, 0)))
                    ckv = jnp.concatenate([cache_swa_k[j].reshape(b, SWA_WINDOW, SWA_KV_W),
                                           cache_swa_v[j].reshape(b, SWA_WINDOW, SWA_KV_W)], axis=-1)
                x3, pool_o, kv_o = _even_mixer(
                    xt[si].reshape(b, s, d), norm_g[l, 1], wts["even_w_in"][j], wts["pool_wbd"][j],
                    wts["pool_scale"][j], wts["swa_sink"][j], wts["even_w_out"][j], cpool, ckv,
                    pos0=0 if prompt else past_len, has_cache=not prompt, name=f"{tags[si]}_even{l}")
                xt[si] = x3.reshape(b * s, d)
                outs[si]["pool"].append(pool_o[:, 1:])
                outs[si]["swa_k"].append(kv_o[..., :SWA_KV_W].reshape(b, SWA_WINDOW, 2, HEAD_DIM))
                outs[si]["swa_v"].append(kv_o[..., SWA_KV_W:].reshape(b, SWA_WINDOW, 2, HEAD_DIM))
            xt = _ffn(xt, norm_g[l, 2], wts["ffn"], (l, 1), name=f"ffn{l}b")
        else:
            lam_init = 0.8 - 0.6 * math.exp(-0.3 * l)
            mixes = []
            for si, (b, s, _) in enumerate(shapes):
                prompt = si == 0
                lc = 128 if prompt else s
                res = _odd_proj(xt[si], norm_g[l, 1], wts["odd_w_in"][j], wts["gmlp_ln_g"][j],
                                wts["gmlp_ln_b"][j], wts["gmlp_wcat"][(j, lc)], wts["gmlp_bias"][(j, lc)],
                                lc=lc, with_vg=not prompt, name=f"{tags[si]}_oddproj{l}")
                q, k, v, kb, vb, dd = res[:6]
                outs[si]["diff_k"].append(k.reshape(b, s, DIFF_HEADS, DIFF_DH))
                outs[si]["diff_v"].append(v.reshape(b, s, DIFF_HEADS, DIFF_DH))
                if not prompt:
                    outs[si]["gmlp_v"].append(res[6].reshape(b, s, GMLP_W))
                qkv = (q.reshape(b, s, DIFF_W), kb.reshape(b, s, DIFF_W), vb.reshape(b, s, DIFF_W))
                if prompt:
                    o = _diff_attn(*qkv, wts["diff_lambda"][j], wts["diff_subln_g"][j],
                                   lam_init=lam_init, name=f"{tags[si]}_diffattn{l}")
                else:
                    o = _diff_attn_cached(*qkv, cache_diff_k, cache_diff_v, j, wts["diff_lambda"][j],
                                          wts["diff_subln_g"][j], lam_init=lam_init,
                                          name=f"{tags[si]}_diffattn{l}")
                mixes.append((o.reshape(b * s, DIFF_W), dd))
            xt = _ffn(xt, norm_g[l, 2], wts["ffn"], (l, 1), mixes=mixes, w_out=wts["odd_w_out"][j],
                      final_g=wts["final_g"] if l == depth - 1 else None, name=f"ffn{l}b")
    return [x.reshape(shape) for x, shape in zip(xt, shapes)], outs


def _prep_weights(norm_g, final_g, ffn_gate, ffn_up, ffn_down, even_w_in, even_w_out, pool_w, pool_scale,
                  swa_sink, odd_w_in, odd_w_out, diff_lambda, diff_subln_g, gmlp_ln_g, gmlp_ln_b,
                  gmlp_w_s, gmlp_b_s, gmlp_chunks):
    ffn = (ffn_gate.astype(BF16), ffn_up.astype(BF16), ffn_down.astype(BF16))
    n_even, n_odd = even_w_in.shape[0], odd_w_in.shape[0]
    wcat, bias = {}, {}
    for j in range(n_odd):
        for lc in gmlp_chunks:
            ws = gmlp_w_s[j][:, :lc, :lc]
            wcat[(j, lc)] = jnp.transpose(ws, (1, 0, 2)).reshape(lc, GMLP_GROUPS * lc)
            bias[(j, lc)] = jnp.repeat(gmlp_b_s[j][:, :lc].T, GMLP_GW, axis=1)
    return dict(
        norm_g=norm_g, final_g=final_g, ffn=ffn,
        even_w_in=[even_w_in[j].astype(BF16) for j in range(n_even)],
        even_w_out=[even_w_out[j].astype(BF16) for j in range(n_even)],
        pool_wbd=[_block_diag(pool_w[j]).astype(BF16) for j in range(n_even)],
        pool_scale=pool_scale, swa_sink=swa_sink,
        odd_w_in=[odd_w_in[j].astype(BF16) for j in range(n_odd)],
        odd_w_out=[odd_w_out[j].astype(BF16) for j in range(n_odd)],
        diff_lambda=diff_lambda, diff_subln_g=diff_subln_g, gmlp_ln_g=gmlp_ln_g, gmlp_ln_b=gmlp_ln_b,
        gmlp_wcat=wcat, gmlp_bias=bias,
    )


def kernel(x_prompt, x_sample, cache_pool, cache_swa_k, cache_swa_v, cache_diff_k, cache_diff_v, norm_g, final_g, ffn_gate, ffn_up, ffn_down, even_w_in, even_w_out, pool_w, pool_scale, swa_sink, odd_w_in, odd_w_out, diff_lambda, diff_subln_g, gmlp_ln_g, gmlp_ln_b, gmlp_w_s, gmlp_b_s):
    wts = _prep_weights(norm_g, final_g, ffn_gate, ffn_up, ffn_down, even_w_in, even_w_out, pool_w,
                        pool_scale, swa_sink, odd_w_in, odd_w_out, diff_lambda, diff_subln_g, gmlp_ln_g,
                        gmlp_ln_b, gmlp_w_s, gmlp_b_s, gmlp_chunks=(128, x_sample.shape[1]))
    caches = (cache_pool, cache_swa_k, cache_swa_v, cache_diff_k, cache_diff_v)
    (y_p, y_s), (out_p, out_s) = _forward((x_prompt, x_sample), caches, wts)
    return (y_p, y_s,
            jnp.stack(out_p["pool"]), jnp.stack(out_s["pool"]),
            jnp.stack(out_p["swa_k"]), jnp.stack(out_s["swa_k"]),
            jnp.stack(out_p["swa_v"]), jnp.stack(out_s["swa_v"]),
            jnp.stack(out_p["diff_k"]), jnp.stack(out_s["diff_k"]),
            jnp.stack(out_p["diff_v"]), jnp.stack(out_s["diff_v"]),
            jnp.stack(out_s["gmlp_v"]))
```

```python
import functools
import math

import jax
import jax.numpy as jnp
from jax import lax
from jax.experimental import pallas as pl
from jax.experimental.pallas import tpu as pltpu

F32 = jnp.float32
BF16 = jnp.bfloat16

EPS = 1e-6
CHUNK = 64
HEAD_DIM = 64
POOL_WINDOWS = (2, 4, 8, 16)
POOL_GW = 96
POOL_W = 4 * POOL_GW
POOL_HALO = 16
SWA_WINDOW = 128
SWA_KEYS = 256
SWA_HEADS = 16
SWA_LOOKAHEAD = 4
SWA_Q_W = SWA_HEADS * HEAD_DIM
SWA_KV_W = 2 * HEAD_DIM
DIFF_HEADS = 8
DIFF_DH = 2 * HEAD_DIM
DIFF_W = DIFF_HEADS * DIFF_DH
DIFF_ACC_ROWS = DIFF_DH + 16
DIFF_LOOKAHEAD = 3
GMLP_GROUPS = 4
GMLP_GW = 96
GMLP_W = GMLP_GROUPS * GMLP_GW
SM_SCALE = HEAD_DIM ** -0.5
LOG2_E = math.log2(math.e)

V7X_VMEM_LIMIT_BYTES = 56 * 1024 * 1024
NEG_INF = float("-inf")


def _rms(x, g):
    return x * lax.rsqrt(jnp.mean(x * x, axis=-1, keepdims=True) + EPS) * g


def _dot(a, b):
    return jnp.dot(a, b, preferred_element_type=F32)


def _resident(shape):
    nd = len(shape)
    return pl.BlockSpec(shape, lambda *_: (0,) * nd, pipeline_mode=pl.Buffered(1))


def _row_tile(n_rows, want):
    t = min(want, n_rows)
    assert n_rows % t == 0, (n_rows, t)
    return t


def _ffn_kernel(*refs, ff_chunk, steps, has_mix, has_final):
    n = len(steps)
    it = iter(refs)
    x_refs = [next(it) for _ in range(n)]
    if has_mix:
        o_refs = [next(it) for _ in range(n)]
        d_refs = [next(it) for _ in range(n)]
        wo_ref = next(it)
    g_ref, wg_ref, wu_ref, wd_ref = next(it), next(it), next(it), next(it)
    if has_final:
        fg_ref = next(it)
    out_refs = [next(it) for _ in range(n)]
    acc_ref = next(it)

    def run(s):
        x = x_refs[s][...]
        if has_mix:
            n_o = o_refs[s].shape[1]
            x = x + _dot(o_refs[s][...], wo_ref[0:n_o, :]) + _dot(d_refs[s][...], wo_ref[n_o:, :])
        hb = _rms(x, g_ref[...]).astype(BF16)
        d_ff = wg_ref.shape[1]
        for c in range(d_ff // ff_chunk):
            sl = slice(c * ff_chunk, (c + 1) * ff_chunk)
            gate = _dot(hb, wg_ref[:, sl])
            up = _dot(hb, wu_ref[:, sl])
            act = (gate * jax.nn.sigmoid(gate) * up).astype(BF16)
            part = _dot(act, wd_ref[sl, :])
            if c == 0:
                acc_ref[...] = part
            else:
                acc_ref[...] += part
        y = x + 0.5 * acc_ref[...]
        if has_final:
            y = _rms(y, fg_ref[...])
        out_refs[s][...] = y

    i = pl.program_id(0)
    first = 0
    for s in range(n):
        pl.when((i >= first) & (i < first + steps[s]))(functools.partial(run, s))
        first += steps[s]


def _stacked(w, idx):
    k = len(idx)
    return pl.BlockSpec((None,) * k + tuple(w.shape[k:]), lambda *_: tuple(idx) + (0,) * (w.ndim - k),
                        pipeline_mode=pl.Buffered(1))


def _ffn(xs, g, ffn_w, layer, mixes=None, w_out=None, final_g=None, *, name):
    d = xs[0].shape[1]
    wg, wu, wd = ffn_w
    d_ff = wg.shape[-1]
    ff_chunk = 256
    assert d_ff % ff_chunk == 0
    tm = _row_tile(min(x.shape[0] for x in xs), 512)
    steps = [x.shape[0] // tm for x in xs]
    starts = [sum(steps[:s]) for s in range(len(xs))]

    def rows(s, w):
        return pl.BlockSpec((tm, w), lambda i: (jnp.clip(i - starts[s], 0, steps[s] - 1), 0))

    args, specs = list(xs), [rows(s, d) for s in range(len(xs))]
    if mixes is not None:
        for k in range(2):
            args += [m[k] for m in mixes]
            specs += [rows(s, m[k].shape[1]) for s, m in enumerate(mixes)]
        args.append(w_out)
        specs.append(_resident(w_out.shape))
    args += [g.reshape(1, d), wg, wu, wd]
    specs += [_resident((1, d)), _stacked(wg, layer), _stacked(wu, layer), _stacked(wd, layer)]
    if final_g is not None:
        args.append(final_g.reshape(1, d))
        specs.append(_resident((1, d)))
    kern = functools.partial(_ffn_kernel, ff_chunk=ff_chunk, steps=tuple(steps), has_mix=mixes is not None,
                             has_final=final_g is not None)
    return list(pl.pallas_call(
        kern,
        grid=(sum(steps),),
        in_specs=specs,
        out_specs=[rows(s, d) for s in range(len(xs))],
        out_shape=[jax.ShapeDtypeStruct(x.shape, F32) for x in xs],
        scratch_shapes=[pltpu.VMEM((tm, d), F32)],
        compiler_params=pltpu.CompilerParams(dimension_semantics=("arbitrary",),
                                             vmem_limit_bytes=V7X_VMEM_LIMIT_BYTES),
        name=name,
    )(*args))


def _even_kernel(sink_ref, x_ref, g_ref, win_ref, pw_ref, ps_ref, wout_ref, cpool_ref, ckv_ref,
                 xo_ref, pool_o_ref, kv_o_ref,
                 e_ref, kv_ref, qb_ref, bo_ref, *, ts, tq, pos0, has_cache):
    sb = pl.program_id(1)
    n_sub = ts // tq
    kv_rows = kv_ref.shape[0]

    @pl.when(sb == 0)
    def _():
        e_ref[0:POOL_HALO, :] = cpool_ref[0]
        kv_ref[0:SWA_WINDOW, :] = ckv_ref[0]
        if kv_rows > SWA_WINDOW + ts:
            kv_ref[SWA_WINDOW + ts:, :] = jnp.zeros((kv_rows - SWA_WINDOW - ts, 2 * SWA_KV_W), F32)

    x = x_ref[0]
    hb = _rms(x, g_ref[...]).astype(BF16)
    u = _dot(hb, win_ref[:, 0:POOL_W])
    qb_ref[...] = (_dot(hb, win_ref[:, POOL_W:POOL_W + SWA_Q_W]) * (SM_SCALE * LOG2_E)).astype(BF16)
    kv_ref[SWA_WINDOW:SWA_WINDOW + ts, :] = _dot(hb, win_ref[:, POOL_W + SWA_Q_W:])
    e_ref[POOL_HALO:POOL_HALO + ts, :] = u

    ev = e_ref[...]
    s2 = ev + pltpu.roll(ev, 1, 0)
    s4 = s2 + pltpu.roll(s2, 2, 0)
    s8 = s4 + pltpu.roll(s4, 4, 0)
    s16 = s8 + pltpu.roll(s8, 8, 0)
    lane = lax.broadcasted_iota(jnp.int32, (1, POOL_W), 1)
    g0, g1, g2 = lane < POOL_GW, lane < 2 * POOL_GW, lane < 3 * POOL_GW
    sw = jnp.where(g0, s2, jnp.where(g1, s4, jnp.where(g2, s8, s16)))[POOL_HALO:]
    win = jnp.where(g0, POOL_WINDOWS[0], jnp.where(g1, POOL_WINDOWS[1],
                                                  jnp.where(g2, POOL_WINDOWS[2], POOL_WINDOWS[3])))
    pos = pos0 + sb * ts + lax.broadcasted_iota(jnp.int32, (ts, 1), 0)
    cnt = jnp.minimum(pos + 1, win).astype(F32)
    diff = sw / cnt - u
    a_out = _dot(diff.astype(BF16), pw_ref[...]) * ps_ref[...]
    new_halo = ev[ts:ts + POOL_HALO]
    e_ref[0:POOL_HALO, :] = new_halo
    pool_o_ref[0] = new_halo

    lane128 = lax.broadcasted_iota(jnp.int32, (1, 2 * HEAD_DIM), 1)
    lo = lane128 < HEAD_DIM
    rq = lax.broadcasted_iota(jnp.int32, (tq, 2 * SWA_KEYS), 0) // CHUNK
    col = lax.broadcasted_iota(jnp.int32, (tq, 2 * SWA_KEYS), 1)
    ek = jnp.where(col >= SWA_KEYS, col - SWA_KEYS, col)
    ekc = ek // CHUNK
    band = (rq <= ekc) & (ekc <= rq + SWA_WINDOW // CHUNK)

    def halves(xa, xb):
        return jnp.concatenate([jnp.where(lo, xa, 0.0), jnp.where(lo, 0.0, xb)], axis=0).astype(BF16)

    row_lo = lax.broadcasted_iota(jnp.int32, (2 * HEAD_DIM, 1), 0) < HEAD_DIM

    def halves_t(xa, xb):
        return jnp.concatenate([jnp.where(row_lo, xa, 0.0), jnp.where(row_lo, 0.0, xb)], axis=1).astype(BF16)

    def sub_block(a):
        r0 = a * tq
        kve = kv_ref[pl.ds(r0, SWA_KEYS), :]
        ke, ve = kve[:, :SWA_KV_W], kve[:, SWA_KV_W:]
        ve_r = pltpu.roll(ve, HEAD_DIM, 1)
        v2 = (halves(ve, ve_r), halves(ve_r, ve))
        ket = ke.T
        ket_r = pltpu.roll(ket, HEAD_DIM, 0)
        k2t = (halves_t(ket, ket_r), halves_t(ket_r, ket))
        if has_cache:
            valid = band
        else:
            valid = band & ((ek >= SWA_WINDOW) | (sb * n_sub + a > 0))
        n_pairs = SWA_HEADS // 2

        def logits(j):
            s = _dot(qb_ref[pl.ds(r0, tq), 128 * j:128 * (j + 1)], k2t[(2 * j) // n_pairs])
            s = jnp.where(valid, s, NEG_INF)
            sinks = [sink_ref[2 * j + half] * LOG2_E for half in range(2)]
            ms = [jnp.maximum(jnp.max(s[:, half * SWA_KEYS:(half + 1) * SWA_KEYS], axis=-1, keepdims=True),
                              sinks[half]) for half in range(2)]
            return s, sinks, ms

        staged = [logits(j) for j in range(SWA_LOOKAHEAD)]
        for j in range(n_pairs):
            if j + SWA_LOOKAHEAD < n_pairs:
                staged.append(logits(j + SWA_LOOKAHEAD))
            s, sinks, ms = staged[j]
            ps, dens = [], []
            for half in range(2):
                p = jnp.exp2(s[:, half * SWA_KEYS:(half + 1) * SWA_KEYS] - ms[half])
                ps.append(p)
                dens.append(jnp.sum(p, axis=-1, keepdims=True) + jnp.exp2(sinks[half] - ms[half]))
            p = jnp.concatenate(ps, axis=1).astype(BF16)
            o = _dot(p, v2[(2 * j) // n_pairs])
            o = o / jnp.where(lo, dens[0], dens[1])
            bo_ref[pl.ds(r0, tq), 128 * j:128 * (j + 1)] = o.astype(BF16)

    for a in range(n_sub):
        sub_block(a)

    y = _dot(a_out.astype(BF16), wout_ref[0:POOL_W, :]) + _dot(bo_ref[...], wout_ref[POOL_W:, :])
    xo_ref[0] = x + y

    new_win = kv_ref[ts:ts + SWA_WINDOW, :]
    kv_ref[0:SWA_WINDOW, :] = new_win
    kv_o_ref[0] = new_win


def _even_mixer(x, g, w_in, pool_wbd, pool_scale, sink, w_out, cache_pool, cache_kv, *, pos0, has_cache, name):
    b, s, d = x.shape
    ts = _row_tile(s, 512)
    tq = min(ts, SWA_WINDOW)
    kv_rows = max(SWA_WINDOW + ts, (ts // tq - 1) * tq + SWA_KEYS)
    kern = functools.partial(_even_kernel, ts=ts, tq=tq, pos0=pos0, has_cache=has_cache)
    per_b = lambda r, w: pl.BlockSpec((1, r, w), lambda bi, si: (bi, 0, 0))
    return pl.pallas_call(
        kern,
        grid=(b, s // ts),
        in_specs=[
            pl.BlockSpec(memory_space=pltpu.SMEM),
            pl.BlockSpec((1, ts, d), lambda bi, si: (bi, si, 0)),
            _resident((1, d)), _resident(w_in.shape), _resident(pool_wbd.shape), _resident((1, POOL_W)),
            _resident(w_out.shape),
            per_b(POOL_HALO, POOL_W), per_b(SWA_WINDOW, 2 * SWA_KV_W),
        ],
        out_specs=[
            pl.BlockSpec((1, ts, d), lambda bi, si: (bi, si, 0)),
            per_b(POOL_HALO, POOL_W), per_b(SWA_WINDOW, 2 * SWA_KV_W),
        ],
        out_shape=[
            jax.ShapeDtypeStruct((b, s, d), F32),
            jax.ShapeDtypeStruct((b, POOL_HALO, POOL_W), F32),
            jax.ShapeDtypeStruct((b, SWA_WINDOW, 2 * SWA_KV_W), F32),
        ],
        scratch_shapes=[
            pltpu.VMEM((POOL_HALO + ts, POOL_W), F32),
            pltpu.VMEM((kv_rows, 2 * SWA_KV_W), F32),
            pltpu.VMEM((ts, SWA_Q_W), BF16),
            pltpu.VMEM((ts, SWA_Q_W), BF16),
        ],
        compiler_params=pltpu.CompilerParams(dimension_semantics=("arbitrary", "arbitrary"),
                                             vmem_limit_bytes=V7X_VMEM_LIMIT_BYTES),
        name=name,
    )(sink, x, g.reshape(1, d), w_in, pool_wbd, pool_scale.reshape(1, POOL_W), w_out, cache_pool, cache_kv)


def _gelu(x):
    return 0.5 * x * (1.0 + lax.erf(x * math.sqrt(0.5)))


def _odd_proj_kernel(*refs, lc, with_vg):
    (x_ref, g_ref, w_ref, lng_ref, lnb_ref, wcat_ref, bm_ref,
     q_ref, k_ref, v_ref, kb_ref, vb_ref, d_ref) = refs[:13]
    vg_ref = refs[13] if with_vg else None
    tm = x_ref.shape[0]

    hb = _rms(x_ref[...], g_ref[...]).astype(BF16)
    q_ref[...] = (_dot(hb, w_ref[:, 0:DIFF_W]) * (SM_SCALE * LOG2_E)).astype(BF16)
    k = _dot(hb, w_ref[:, DIFF_W:2 * DIFF_W])
    k_ref[...] = k
    kb_ref[...] = k.astype(BF16)
    v = _dot(hb, w_ref[:, 2 * DIFF_W:3 * DIFF_W])
    v_ref[...] = v
    vb_ref[...] = v.astype(BF16)

    u = _gelu(_dot(hb, w_ref[:, 3 * DIFF_W:3 * DIFF_W + GMLP_W]))
    gv = _gelu(_dot(hb, w_ref[:, 3 * DIFF_W + GMLP_W:]))
    mu = jnp.mean(gv, axis=-1, keepdims=True)
    cen = gv - mu
    var = jnp.mean(cen * cen, axis=-1, keepdims=True)
    vg = cen * lax.rsqrt(var + EPS) * lng_ref[...] + lnb_ref[...]
    if with_vg:
        vg_ref[...] = vg

    t_i = lax.broadcasted_iota(jnp.int32, (lc, GMLP_GROUPS * lc), 0)
    s_i = lax.broadcasted_iota(jnp.int32, (lc, GMLP_GROUPS * lc), 1) % lc
    wcat = jnp.where(s_i <= t_i, wcat_ref[...], 0.0).astype(BF16)
    lane = lax.broadcasted_iota(jnp.int32, (1, GMLP_W), 1) // GMLP_GW
    for c in range(tm // lc):
        rows = slice(c * lc, (c + 1) * lc)
        vc = vg[rows]
        vstack = jnp.concatenate([jnp.where(lane == gi, vc, 0.0) for gi in range(GMLP_GROUPS)],
                                 axis=0).astype(BF16)
        mix = _dot(wcat, vstack) + bm_ref[...]
        d_ref[rows, :] = (u[rows] * mix).astype(BF16)


def _odd_proj(x, g, w_in, ln_g, ln_b, wcat, bias_m, *, lc, with_vg, name):
    t, d = x.shape
    tm = _row_tile(t, 512)
    assert tm % lc == 0
    row = lambda w: pl.BlockSpec((tm, w), lambda i: (i, 0))
    out_shape = [
        jax.ShapeDtypeStruct((t, DIFF_W), BF16),
        jax.ShapeDtypeStruct((t, DIFF_W), F32),
        jax.ShapeDtypeStruct((t, DIFF_W), F32),
        jax.ShapeDtypeStruct((t, DIFF_W), BF16),
        jax.ShapeDtypeStruct((t, DIFF_W), BF16),
        jax.ShapeDtypeStruct((t, GMLP_W), BF16),
    ]
    out_specs = [row(DIFF_W)] * 5 + [row(GMLP_W)]
    if with_vg:
        out_shape.append(jax.ShapeDtypeStruct((t, GMLP_W), F32))
        out_specs.append(row(GMLP_W))
    return pl.pallas_call(
        functools.partial(_odd_proj_kernel, lc=lc, with_vg=with_vg),
        grid=(t // tm,),
        in_specs=[row(d), _resident((1, d)), _resident(w_in.shape), _resident((1, GMLP_W)),
                  _resident((1, GMLP_W)), _resident(wcat.shape), _resident(bias_m.shape)],
        out_specs=out_specs,
        out_shape=out_shape,
        compiler_params=pltpu.CompilerParams(dimension_semantics=("arbitrary",),
                                             vmem_limit_bytes=V7X_VMEM_LIMIT_BYTES),
        name=name,
    )(x, g.reshape(1, d), w_in, ln_g.reshape(1, GMLP_W), ln_b.reshape(1, GMLP_W), wcat, bias_m)


def _head_lanes(g):
    return slice(g * DIFF_DH, (g + 1) * DIFF_DH)


def _diff_lambda(lam_ref, lam_init):
    lp = lam_ref[...]
    return (jnp.exp(jnp.sum(lp[0:1] * lp[1:2], axis=-1, keepdims=True))
            - jnp.exp(jnp.sum(lp[2:3] * lp[3:4], axis=-1, keepdims=True)) + lam_init)


def _diff_diag_mask(tq):
    return (lax.broadcasted_iota(jnp.int32, (tq, 2 * tq), 0) // CHUNK
            <= (lax.broadcasted_iota(jnp.int32, (tq, 2 * tq), 1) % tq) // CHUNK)


def _diff_stage_queries(qs_ref, g, qi, tq):
    d_lo = lax.broadcasted_iota(jnp.int32, (DIFF_DH, 2 * tq), 0) < HEAD_DIM
    c_lo = lax.broadcasted_iota(jnp.int32, (DIFF_DH, 2 * tq), 1) < tq
    qt = jnp.concatenate([qi, qi], axis=0).astype(F32).T
    qs_ref[g] = jnp.where(d_lo == c_lo, qt, 0.0).astype(BF16)


def _diff_start(state, g, qi, tq):
    qs_ref, m_ref, acc_ref = state
    _diff_stage_queries(qs_ref, g, qi, tq)
    m_ref[g] = jnp.full((1, 2 * tq), NEG_INF, F32)
    acc_ref[g] = jnp.zeros((DIFF_ACC_ROWS, 2 * tq), F32)


def _diff_scores(state, g, kb):
    return _dot(kb, state[0][g])


def _diff_update(state, g, s, vt, mask):
    _, m_ref, acc_ref = state
    if mask is not None:
        s = jnp.where(mask, s, NEG_INF)
    m_prev = m_ref[g]
    m_new = jnp.maximum(m_prev, jnp.max(s, axis=0, keepdims=True))
    alpha = jnp.exp2(m_prev - m_new)
    p = jnp.exp2(s - m_new)
    acc_ref[g] = alpha * acc_ref[g] + _dot(vt, p.astype(BF16))
    m_ref[g] = m_new


def _diff_heads_step(state, n_heads, key_block, value_block, mask):
    staged = [_diff_scores(state, g, key_block(g)) for g in range(min(DIFF_LOOKAHEAD, n_heads))]
    for g in range(n_heads):
        if g + DIFF_LOOKAHEAD < n_heads:
            staged.append(_diff_scores(state, g + DIFF_LOOKAHEAD, key_block(g + DIFF_LOOKAHEAD)))
        _diff_update(state, g, staged[g], value_block(g), mask)


def _diff_finish(state, g, lam, sg, tq, lam_init):
    acc = state[-1][g]
    ot = acc[:DIFF_DH] / acc[DIFF_DH:DIFF_DH + 1]
    dt = ot - lam * pltpu.roll(ot, tq, 1)
    dt = dt * lax.rsqrt(jnp.mean(dt * dt, axis=0, keepdims=True) + EPS) * sg
    return ((dt * (1.0 - lam_init)).T[:tq]).astype(BF16)


def _vt(v):
    keys = v.shape[0]
    ones_row = lax.broadcasted_iota(jnp.int32, (DIFF_ACC_ROWS - DIFF_DH, keys), 0) == 0
    return jnp.concatenate([v.astype(F32).T.astype(BF16), jnp.where(ones_row, 1.0, 0.0).astype(BF16)], axis=0)


def _diff_attn_kernel(lam_ref, sg_ref, q_ref, k_ref, v_ref, o_ref, vt_ref, s_ref, m_ref, a_ref,
                      qs_ref, acc_ref, *, tq, n_heads, lam_init):
    FIRST = 2
    nq = q_ref.shape[1] // tq
    heads = range(n_heads)
    state = (qs_ref, None, acc_ref)
    lam = _diff_lambda(lam_ref, lam_init)
    diag_ok = _diff_diag_mask(tq)

    for g in heads:
        for c in range(nq):
            vt_ref[g, c] = _vt(v_ref[0, c * tq:(c + 1) * tq, _head_lanes(g)])

    def rows(blk):
        return blk * tq if isinstance(blk, int) else pl.multiple_of(blk * tq, tq)

    def logits_phase(g, r0, src, dst, mask):
        s = _dot(k_ref[0, pl.ds(r0, tq), _head_lanes(g)], qs_ref[g])
        if mask is not None:
            s = jnp.where(mask, s, NEG_INF)
        s_ref[dst, g] = s
        col_max = jnp.max(s, axis=0, keepdims=True)
        if src is None:
            m_ref[dst, g] = col_max
        else:
            m_old = m_ref[src, g]
            m_new = jnp.maximum(m_old, col_max)
            m_ref[dst, g] = m_new
            a_ref[dst, g] = jnp.exp2(m_old - m_new)

    def accum_phase(g, blk, slot, first=False):
        p = jnp.exp2(s_ref[slot, g] - m_ref[slot, g])
        pv = _dot(vt_ref[g, blk], p.astype(BF16))
        acc_ref[g] = pv if first else a_ref[slot, g] * acc_ref[g] + pv

    def stage(i, mask):
        for g in heads:
            _diff_stage_queries(qs_ref, g, q_ref[0, pl.ds(rows(i), tq), _head_lanes(g)], tq)
            logits_phase(g, 0, None, FIRST, mask)

    def step(blk, slot, dst, mask):
        r1 = rows(blk + 1)
        logits_phase(0, r1, slot, dst, mask)
        for g in heads:
            if g + 1 < n_heads:
                logits_phase(g + 1, r1, slot, dst, mask)
            accum_phase(g, blk, slot, first=slot == FIRST)

    def two_steps(t, c):
        step(2 * t + 1, 0, 1, None)
        step(2 * t + 2, 1, 0, None)
        return c

    def q_block(i, odd):
        if isinstance(i, int) and i == 0:
            last_slot = FIRST
        elif odd == "one":
            step(0, FIRST, 0, diag_ok)
            last_slot = 0
        else:
            step(0, FIRST, 0, None)
            if odd:
                lax.fori_loop(0, (i - 3) // 2, two_steps, 0)
                step(i - 2, 0, 1, None)
                step(i - 1, 1, 0, diag_ok)
                last_slot = 0
            else:
                lax.fori_loop(0, (i - 2) // 2, two_steps, 0)
                step(i - 1, 0, 1, diag_ok)
                last_slot = 1
        nxt = min(i + 1, nq - 1) if isinstance(i, int) else jnp.minimum(i + 1, nq - 1)
        if last_slot == FIRST:
            for g in heads:
                accum_phase(g, i, last_slot, first=True)
            stage(nxt, None)
        else:
            stage(nxt, None)
            for g in heads:
                accum_phase(g, i, last_slot)
        for g in heads:
            o_ref[0, pl.ds(rows(i), tq), _head_lanes(g)] = _diff_finish(state, g, lam, sg_ref[...], tq, lam_init)

    stage(0, diag_ok)
    q_block(0, False)
    if nq > 1:
        def q_one(i, c):
            q_block(i, "one")
            return c
        lax.fori_loop(1, jnp.minimum(pl.num_programs(0) + pl.program_id(0), 2), q_one, 0)

    def q_pair(u, c):
        q_block(2 * u + 2, False)
        q_block(2 * u + 3, True)
        return c
    lax.fori_loop(0, (nq - 2) // 2, q_pair, 0)
    if nq > 2 and nq % 2 == 1:
        q_block(nq - 1, False)


def _diff_attn_cached_kernel(lam_ref, sg_ref, q_ref, k_ref, v_ref, ck_ref, cv_ref, o_ref, *state, tq, lam_init):
    c = pl.program_id(1)
    heads = range(DIFF_HEADS)

    @pl.when(c == 0)
    def _():
        for g in heads:
            _diff_start(state, g, q_ref[0, :, _head_lanes(g)], tq)

    tkc = ck_ref.shape[0] // DIFF_HEADS
    head_rows = lambda g: pl.ds(g, tkc, stride=DIFF_HEADS)
    cached_keys = lambda g: ck_ref[head_rows(g), :].astype(BF16)
    _diff_heads_step(state, DIFF_HEADS, cached_keys, lambda g: _vt(cv_ref[head_rows(g), :]), None)

    @pl.when(c == pl.num_programs(1) - 1)
    def _():
        lam = _diff_lambda(lam_ref, lam_init)
        new_keys = lambda g: k_ref[0, :, _head_lanes(g)]
        _diff_heads_step(state, DIFF_HEADS, new_keys, lambda g: _vt(v_ref[0, :, _head_lanes(g)]),
                         _diff_diag_mask(tq))
        for g in heads:
            o_ref[0, :, _head_lanes(g)] = _diff_finish(state, g, lam, sg_ref[...], tq, lam_init)


def _diff_state_scratch(n_heads, tq):
    return [pltpu.VMEM((n_heads, DIFF_DH, 2 * tq), BF16),
            pltpu.VMEM((n_heads, 1, 2 * tq), F32),
            pltpu.VMEM((n_heads, DIFF_ACC_ROWS, 2 * tq), F32)]


def _diff_attn(q, kb, vb, lam_p, subln_g, *, lam_init, name):
    b, s, _ = q.shape
    tq = _row_tile(s, 256)
    n_heads = 4
    grp = pl.BlockSpec((1, s, n_heads * DIFF_DH), lambda bi, hi: (bi, 0, hi))
    return pl.pallas_call(
        functools.partial(_diff_attn_kernel, tq=tq, n_heads=n_heads, lam_init=lam_init),
        grid=(b, DIFF_HEADS // n_heads),
        in_specs=[_resident(lam_p.shape), _resident((DIFF_DH, 1)), grp, grp, grp],
        out_specs=grp,
        out_shape=jax.ShapeDtypeStruct((b, s, DIFF_W), BF16),
        scratch_shapes=[pltpu.VMEM((n_heads, s // tq, DIFF_ACC_ROWS, tq), BF16),
                        pltpu.VMEM((3, n_heads, tq, 2 * tq), F32),
                        pltpu.VMEM((3, n_heads, 1, 2 * tq), F32),
                        pltpu.VMEM((3, n_heads, 1, 2 * tq), F32),
                        pltpu.VMEM((n_heads, DIFF_DH, 2 * tq), BF16),
                        pltpu.VMEM((n_heads, DIFF_ACC_ROWS, 2 * tq), F32)],
        compiler_params=pltpu.CompilerParams(dimension_semantics=("arbitrary", "arbitrary"),
                                             vmem_limit_bytes=V7X_VMEM_LIMIT_BYTES),
        name=name,
    )(lam_p, subln_g.reshape(DIFF_DH, 1), q, kb, vb)


def _diff_attn_cached(q, kb, vb, cache_k, cache_v, layer_j, lam_p, subln_g, *, lam_init, name):
    b, s, _ = q.shape
    assert s % CHUNK == 0 and s <= 256, s
    n_layers, _, lc = cache_k.shape[:3]
    tkc = _row_tile(lc, 1024)
    cache_k, cache_v = (c.reshape(n_layers, b, lc * DIFF_HEADS, DIFF_DH) for c in (cache_k, cache_v))
    new = pl.BlockSpec((1, s, DIFF_W), lambda bi, ci: (bi, 0, 0))
    cache = pl.BlockSpec((None, None, tkc * DIFF_HEADS, DIFF_DH), lambda bi, ci: (layer_j, bi, ci, 0))
    return pl.pallas_call(
        functools.partial(_diff_attn_cached_kernel, tq=s, lam_init=lam_init),
        grid=(b, lc // tkc),
        in_specs=[_resident(lam_p.shape), _resident((DIFF_DH, 1)), new, new, new, cache, cache],
        out_specs=new,
        out_shape=jax.ShapeDtypeStruct((b, s, DIFF_W), BF16),
        scratch_shapes=_diff_state_scratch(DIFF_HEADS, s),
        compiler_params=pltpu.CompilerParams(dimension_semantics=("arbitrary", "arbitrary"),
                                             vmem_limit_bytes=V7X_VMEM_LIMIT_BYTES),
        name=name,
    )(lam_p, subln_g.reshape(DIFF_DH, 1), q, kb, vb, cache_k, cache_v)


def _block_diag(w):
    g, a, bb = w.shape
    eye = jnp.eye(g, dtype=w.dtype)
    return (eye[:, None, :, None] * w[:, :, None, :]).reshape(g * a, g * bb)


def _forward(xs, caches, wts):
    cache_pool, cache_swa_k, cache_swa_v, cache_diff_k, cache_diff_v = caches
    shapes = [x.shape for x in xs]
    d = shapes[0][2]
    norm_g = wts["norm_g"]
    depth = norm_g.shape[0]
    assert depth % 2 == 0
    past_len = cache_diff_k.shape[2]
    tags = ("p", "s")
    xt = [x.reshape(-1, d) for x in xs]
    outs = [dict(pool=[], swa_k=[], swa_v=[], diff_k=[], diff_v=[], gmlp_v=[]) for _ in xs]

    for l in range(depth):
        j = l // 2
        xt = _ffn(xt, norm_g[l, 0], wts["ffn"], (l, 0), name=f"ffn{l}a")
        if l % 2 == 0:
            for si, (b, s, _) in enumerate(shapes):
                prompt = si == 0
                if prompt:
                    cpool = jnp.zeros((b, POOL_HALO, POOL_W), F32)
                    ckv = jnp.zeros((b, SWA_WINDOW, 2 * SWA_KV_W), F32)
                else:
                    cpool = jnp.pad(cache_pool[j], ((0, 0), (POOL_HALO - cache_pool.shape[2], 0), (0, 0)))
                    ckv = jnp.concatenate([cache_swa_k[j].reshape(b, SWA_WINDOW, SWA_KV_W),
                                           cache_swa_v[j].reshape(b, SWA_WINDOW, SWA_KV_W)], axis=-1)
                x3, pool_o, kv_o = _even_mixer(
                    xt[si].reshape(b, s, d), norm_g[l, 1], wts["even_w_in"][j], wts["pool_wbd"][j],
                    wts["pool_scale"][j], wts["swa_sink"][j], wts["even_w_out"][j], cpool, ckv,
                    pos0=0 if prompt else past_len, has_cache=not prompt, name=f"{tags[si]}_even{l}")
                xt[si] = x3.reshape(b * s, d)
                outs[si]["pool"].append(pool_o[:, 1:])
                outs[si]["swa_k"].append(kv_o[..., :SWA_KV_W].reshape(b, SWA_WINDOW, 2, HEAD_DIM))
                outs[si]["swa_v"].append(kv_o[..., SWA_KV_W:].reshape(b, SWA_WINDOW, 2, HEAD_DIM))
            xt = _ffn(xt, norm_g[l, 2], wts["ffn"], (l, 1), name=f"ffn{l}b")
        else:
            lam_init = 0.8 - 0.6 * math.exp(-0.3 * l)
            mixes = []
            for si, (b, s, _) in enumerate(shapes):
                prompt = si == 0
                lc = 128 if prompt else s
                res = _odd_proj(xt[si], norm_g[l, 1], wts["odd_w_in"][j], wts["gmlp_ln_g"][j],
                                wts["gmlp_ln_b"][j], wts["gmlp_wcat"][(j, lc)], wts["gmlp_bias"][(j, lc)],
                                lc=lc, with_vg=not prompt, name=f"{tags[si]}_oddproj{l}")
                q, k, v, kb, vb, dd = res[:6]
                outs[si]["diff_k"].append(k.reshape(b, s, DIFF_HEADS, DIFF_DH))
                outs[si]["diff_v"].append(v.reshape(b, s, DIFF_HEADS, DIFF_DH))
                if not prompt:
                    outs[si]["gmlp_v"].append(res[6].reshape(b, s, GMLP_W))
                qkv = (q.reshape(b, s, DIFF_W), kb.reshape(b, s, DIFF_W), vb.reshape(b, s, DIFF_W))
                if prompt:
                    o = _diff_attn(*qkv, wts["diff_lambda"][j], wts["diff_subln_g"][j],
                                   lam_init=lam_init, name=f"{tags[si]}_diffattn{l}")
                else:
                    o = _diff_attn_cached(*qkv, cache_diff_k, cache_diff_v, j, wts["diff_lambda"][j],
                                          wts["diff_subln_g"][j], lam_init=lam_init,
                                          name=f"{tags[si]}_diffattn{l}")
                mixes.append((o.reshape(b * s, DIFF_W), dd))
            xt = _ffn(xt, norm_g[l, 2], wts["ffn"], (l, 1), mixes=mixes, w_out=wts["odd_w_out"][j],
                      final_g=wts["final_g"] if l == depth - 1 else None, name=f"ffn{l}b")
    return [x.reshape(shape) for x, shape in zip(xt, shapes)], outs


def _prep_weights(norm_g, final_g, ffn_gate, ffn_up, ffn_down, even_w_in, even_w_out, pool_w, pool_scale,
                  swa_sink, odd_w_in, odd_w_out, diff_lambda, diff_subln_g, gmlp_ln_g, gmlp_ln_b,
                  gmlp_w_s, gmlp_b_s, gmlp_chunks):
    ffn = (ffn_gate.astype(BF16), ffn_up.astype(BF16), ffn_down.astype(BF16))
    n_even, n_odd = even_w_in.shape[0], odd_w_in.shape[0]
    wcat, bias = {}, {}
    for j in range(n_odd):
        for lc in gmlp_chunks:
            ws = gmlp_w_s[j][:, :lc, :lc]
            wcat[(j, lc)] = jnp.transpose(ws, (1, 0, 2)).reshape(lc, GMLP_GROUPS * lc)
            bias[(j, lc)] = jnp.repeat(gmlp_b_s[j][:, :lc].T, GMLP_GW, axis=1)
    return dict(
        norm_g=norm_g, final_g=final_g, ffn=ffn,
        even_w_in=[even_w_in[j].astype(BF16) for j in range(n_even)],
        even_w_out=[even_w_out[j].astype(BF16) for j in range(n_even)],
        pool_wbd=[_block_diag(pool_w[j]).astype(BF16) for j in range(n_even)],
        pool_scale=pool_scale, swa_sink=swa_sink,
        odd_w_in=[odd_w_in[j].astype(BF16) for j in range(n_odd)],
        odd_w_out=[odd_w_out[j].astype(BF16) for j in range(n_odd)],
        diff_lambda=diff_lambda, diff_subln_g=diff_subln_g, gmlp_ln_g=gmlp_ln_g, gmlp_ln_b=gmlp_ln_b,
        gmlp_wcat=wcat, gmlp_bias=bias,
    )


def kernel(x_prompt, x_sample, cache_pool, cache_swa_k, cache_swa_v, cache_diff_k, cache_diff_v, norm_g, final_g, ffn_gate, ffn_up, ffn_down, even_w_in, even_w_out, pool_w, pool_scale, swa_sink, odd_w_in, odd_w_out, diff_lambda, diff_subln_g, gmlp_ln_g, gmlp_ln_b, gmlp_w_s, gmlp_b_s):
    wts = _prep_weights(norm_g, final_g, ffn_gate, ffn_up, ffn_down, even_w_in, even_w_out, pool_w,
                        pool_scale, swa_sink, odd_w_in, odd_w_out, diff_lambda, diff_subln_g, gmlp_ln_g,
                        gmlp_ln_b, gmlp_w_s, gmlp_b_s, gmlp_chunks=(128, x_sample.shape[1]))
    caches = (cache_pool, cache_swa_k, cache_swa_v, cache_diff_k, cache_diff_v)
    (y_p, y_s), (out_p, out_s) = _forward((x_prompt, x_sample), caches, wts)
    return (y_p, y_s,
            jnp.stack(out_p["pool"]), jnp.stack(out_s["pool"]),
            jnp.stack(out_p["swa_k"]), jnp.stack(out_s["swa_k"]),
            jnp.stack(out_p["swa_v"]), jnp.stack(out_s["swa_v"]),
            jnp.stack(out_p["diff_k"]), jnp.stack(out_s["diff_k"]),
            jnp.stack(out_p["diff_v"]), jnp.stack(out_s["diff_v"]),
            jnp.stack(out_s["gmlp_v"]))
```
